```python
import jax, jax.numpy as jnp
from jax import lax
import numpy as np

D_MODEL = 1024
BATCH = 4
SEQ = 8192
DEPTH = 2

CHUNK = 64
MIX_WIDTH = D_MODEL
CONV_WIDTH = MIX_WIDTH // 2
CONV_KERNEL = 31
HGRN_WIDTH = MIX_WIDTH - CONV_WIDTH
HGRN_HEAD_DIM = 128
HGRN_HEADS = HGRN_WIDTH // HGRN_HEAD_DIM
IN_WIDTH = 2 * CONV_WIDTH + 4 * HGRN_WIDTH
D_FF = -(-8 * D_MODEL // (3 * 256)) * 256
N_MOD = 6
EPS = 1e-6

kernel_name = "hymba_conformer_hgrn2_adaln_block"


def rms_norm(x, g):
    xf = x.astype(jnp.float32)
    y = xf * lax.rsqrt(jnp.mean(xf * xf, axis=-1, keepdims=True) + EPS)
    return (y * g.astype(jnp.float32)).astype(x.dtype)


def layer_norm(x, g, b):
    xf = x.astype(jnp.float32)
    mu = jnp.mean(xf, axis=-1, keepdims=True)
    var = jnp.mean(jnp.square(xf - mu), axis=-1, keepdims=True)
    y = (xf - mu) * lax.rsqrt(var + EPS)
    return (y * g.astype(jnp.float32) + b.astype(jnp.float32)).astype(x.dtype)


def conformer_conv(val, gate, w, b, ln_g, ln_b):
    u = val * jax.nn.sigmoid(gate)
    y = lax.conv_general_dilated(
        u, w[:, None, :].astype(u.dtype), window_strides=(1,),
        padding=[(CONV_KERNEL - 1, 0)],
        dimension_numbers=('NWC', 'WIO', 'NWC'),
        feature_group_count=u.shape[-1]) + b.astype(u.dtype)
    return jax.nn.silu(layer_norm(y, ln_g, ln_b))


def hgrn2(q_raw, f_raw, i_raw, g_raw, lb, norm_g):
    B, S, _ = q_raw.shape
    n_chunks = S // CHUNK
    f32 = jnp.float32
    q = jax.nn.silu(q_raw.astype(f32))
    z = f_raw.astype(f32)
    lb = lb.astype(f32)
    log_f = jnp.logaddexp(jnp.log(lb), jnp.log1p(-lb) + jax.nn.log_sigmoid(z))
    k = (1.0 - lb) * jax.nn.sigmoid(-z)
    v = i_raw.astype(f32)

    def to_chunks(t):
        return t.reshape(B, n_chunks, CHUNK, HGRN_HEADS, HGRN_HEAD_DIM).transpose(1, 0, 3, 2, 4)

    qc, kc, vc = to_chunks(q), to_chunks(k), to_chunks(v)
    bc = jnp.cumsum(to_chunks(log_f), axis=3)
    mask = jnp.tril(jnp.ones((CHUNK, CHUNK), dtype=bool))[:, :, None]

    def step(state, inp):
        q_, k_, v_, b_ = inp
        rel = jnp.where(mask, b_[:, :, :, None, :] - b_[:, :, None, :, :], -jnp.inf)
        scores = jnp.einsum('bhtsk,bhsk->bhts', jnp.exp(rel) * q_[:, :, :, None, :], k_)
        o = (jnp.einsum('bhts,bhsv->bhtv', scores, v_)
             + jnp.einsum('bhtk,bhkv->bhtv', q_ * jnp.exp(b_), state))
        b_end = b_[:, :, -1:, :]
        state = (jnp.exp(b_end[:, :, 0, :])[..., None] * state
                 + jnp.einsum('bhsk,bhsv->bhkv', k_ * jnp.exp(b_end - b_), v_))
        return state, o

    s0 = jnp.zeros((B, HGRN_HEADS, HGRN_HEAD_DIM, HGRN_HEAD_DIM), f32)
    _, o = lax.scan(step, s0, (qc, kc, vc, bc))
    o = o.transpose(1, 0, 3, 2, 4).reshape(B, S, HGRN_HEADS, HGRN_HEAD_DIM)
    o = rms_norm(o, norm_g).reshape(B, S, HGRN_WIDTH)
    return (o * jax.nn.silu(g_raw.astype(f32))).astype(q_raw.dtype)


def setup_inputs(seed: int = 0) -> dict:
    key = jax.random.key(seed)
    ks = jax.random.split(key, 20)
    nrm = lambda k, shape, s: jax.random.normal(k, shape, jnp.float32) * s
    gain = lambda k, shape: 1.0 + nrm(k, shape, 0.05)
    return {
        "x": nrm(ks[0], (BATCH, SEQ, D_MODEL), 1.0),
        "c": nrm(ks[1], (BATCH, D_MODEL), 1.0),
        "w_ada": nrm(ks[2], (DEPTH, D_MODEL, N_MOD * D_MODEL), D_MODEL ** -0.5),
        "b_ada": nrm(ks[3], (DEPTH, N_MOD * D_MODEL), 0.02),
        "g_pre_mix": gain(ks[4], (DEPTH, D_MODEL)),
        "g_post_mix": gain(ks[5], (DEPTH, D_MODEL)),
        "w_in": nrm(ks[6], (DEPTH, D_MODEL, IN_WIDTH), D_MODEL ** -0.5),
        "conv_w": nrm(ks[7], (DEPTH, CONV_KERNEL, CONV_WIDTH), CONV_KERNEL ** -0.5),
        "conv_b": nrm(ks[8], (DEPTH, CONV_WIDTH), 0.02),
        "conv_ln_g": gain(ks[9], (DEPTH, CONV_WIDTH)),
        "conv_ln_b": nrm(ks[10], (DEPTH, CONV_WIDTH), 0.02),
        "hgrn_lower_bounds": nrm(ks[11], (DEPTH, HGRN_WIDTH), 0.1),
        "hgrn_norm_g": gain(ks[12], (DEPTH, HGRN_HEAD_DIM)),
        "w_out": nrm(ks[13], (DEPTH, MIX_WIDTH, D_MODEL), MIX_WIDTH ** -0.5),
        "g_pre_ffn": gain(ks[14], (DEPTH, D_MODEL)),
        "g_post_ffn": gain(ks[15], (DEPTH, D_MODEL)),
        "w_ffn_in": nrm(ks[16], (DEPTH, D_MODEL, 2 * D_FF), D_MODEL ** -0.5),
        "w_ffn_out": nrm(ks[17], (DEPTH, D_FF, D_MODEL), D_FF ** -0.5),
    }


def reference(x, c, w_ada, b_ada, g_pre_mix, g_post_mix, w_in, conv_w, conv_b,
              conv_ln_g, conv_ln_b, hgrn_lower_bounds, hgrn_norm_g, w_out,
              g_pre_ffn, g_post_ffn, w_ffn_in, w_ffn_out):
    lb_all = jnp.cumsum(jax.nn.softmax(hgrn_lower_bounds.astype(jnp.float32), axis=0), axis=0)
    lb_all = lb_all - lb_all[0:1]
    cond = jax.nn.silu(c)
    splits = [CONV_WIDTH, 2 * CONV_WIDTH,
              2 * CONV_WIDTH + HGRN_WIDTH,
              2 * CONV_WIDTH + 2 * HGRN_WIDTH,
              2 * CONV_WIDTH + 3 * HGRN_WIDTH]
    for l in range(DEPTH):
        mod = cond @ w_ada[l] + b_ada[l]
        sh1, sc1, gt1, sh2, sc2, gt2 = [m[:, None, :] for m in jnp.split(mod, N_MOD, axis=-1)]

        h = rms_norm(x, g_pre_mix[l]) * (1.0 + sc1) + sh1
        zin = h @ w_in[l]
        cv_val, cv_gate, q, f, i, og = jnp.split(zin, splits, axis=-1)
        out_a = conformer_conv(cv_val, cv_gate, conv_w[l], conv_b[l], conv_ln_g[l], conv_ln_b[l])
        out_b = hgrn2(q, f, i, og, lb_all[l], hgrn_norm_g[l])
        mix = jnp.concatenate([out_a, out_b], axis=-1) @ w_out[l]
        x = x + gt1 * rms_norm(mix, g_post_mix[l])

        h2 = rms_norm(x, g_pre_ffn[l]) * (1.0 + sc2) + sh2
        gate, up = jnp.split(h2 @ w_ffn_in[l], 2, axis=-1)
        y = (jax.nn.silu(gate) * up) @ w_ffn_out[l]
        x = x + gt2 * rms_norm(y, g_post_ffn[l])
    return x
```

```python
import functools

import numpy as np
import jax
import jax.numpy as jnp
from jax import lax
from jax.experimental import pallas as pl
from jax.experimental.pallas import tpu as pltpu

EPS = 1e-6
CONV_KERNEL = 31
CONV_HALO = 32
HEAD_DIM = 128
CHUNK = 64
LEVELS = (1, 2, 4, 8, 16, 32)
N_MOD = 6
VMEM_LIMIT_BYTES = 56 * 1024 * 1024

_NT = (((1,), (1,)), ((), ()))
_TN = (((0,), (0,)), ((), ()))


def _decay_matrix():
    t = np.arange(CHUNK)[:, None]
    j = np.arange(CHUNK)[None, :]
    blocks = [(j <= t), (j > t)]
    for n in LEVELS:
        p = (t // (2 * n)) * (2 * n) + n - 1
        upper = (t % (2 * n)) >= n
        blocks.append(np.where(upper, (j > p) & (j <= t), (j > t) & (j <= p)))
    z = np.concatenate(blocks, axis=0).astype(np.float32)
    return np.concatenate([z, z], axis=1)


def _sigmoid(x):
    return 1.0 / (1.0 + jnp.exp(-x))


def _silu(x):
    return x * _sigmoid(x)


def _rms(x, g):
    return x * lax.rsqrt(jnp.mean(x * x, axis=-1, keepdims=True) + EPS) * g


def _mod_kernel(c_ref, w_ref, b_ref, o_ref):
    c = c_ref[...]
    o_ref[0] = jnp.dot(_silu(c), w_ref[0], preferred_element_type=jnp.float32) + b_ref[0]


def _mixer_kernel(x_ref, mod_ref, gpre_ref, gpost_ref, win_ref, cw_ref, cb_ref, lng_ref, lnb_ref,
                  lb_ref, ng_ref, wout_ref, z_ref, o_ref, zin_s, ubuf_s, st_s, mix_s, *, layer, depth):
    tile = x_ref.shape[1]
    cw = CONV_HALO
    conv_width = cb_ref.shape[1]
    hgrn_width = lb_ref.shape[1]
    heads = hgrn_width // HEAD_DIM
    f32, bf16 = jnp.float32, jnp.bfloat16

    @pl.when(pl.program_id(1) == 0)
    def _():
        ubuf_s[0:cw, :] = jnp.zeros((cw, conv_width), f32)
        st_s[...] = jnp.zeros(st_s.shape, f32)

    x = x_ref[0]
    mod = mod_ref[0]
    sh1, sc1, gt1 = mod[0:1], mod[1:2], mod[2:3]
    h = _rms(x, gpre_ref[...]) * (1.0 + sc1) + sh1
    zin_s[...] = jnp.dot(h.astype(bf16), win_ref[...], preferred_element_type=f32)

    u = zin_s[:, 0:conv_width] * _sigmoid(zin_s[:, conv_width:2 * conv_width])
    ubuf_s[cw:cw + tile, :] = u
    rb = 32
    for r in range(tile // rb):
        acc = jnp.broadcast_to(cb_ref[...], (rb, conv_width))
        for j in range(CONV_KERNEL):
            start = cw - (CONV_KERNEL - 1) + j + r * rb
            acc = acc + cw_ref[j:j + 1, :] * ubuf_s[start:start + rb, :]
        mu = jnp.mean(acc, axis=-1, keepdims=True)
        d = acc - mu
        var = jnp.mean(d * d, axis=-1, keepdims=True)
        y = d * lax.rsqrt(var + EPS) * lng_ref[...] + lnb_ref[...]
        mix_s[r * rb:(r + 1) * rb, 0:conv_width] = _silu(y).astype(bf16)
    ubuf_s[0:cw, :] = ubuf_s[tile:tile + cw, :]

    rows = [lb_ref[i:i + 1, :] for i in range(depth)]
    m = functools.reduce(jnp.maximum, rows)
    e = [jnp.exp(r_ - m) for r_ in rows]
    tot = functools.reduce(jnp.add, e)
    lb = functools.reduce(jnp.add, e[1:layer + 1], jnp.zeros_like(tot)) / tot
    log_lb = jnp.log(lb)
    log_1m = jnp.log1p(-lb)

    o0 = 2 * conv_width
    tt = lax.broadcasted_iota(jnp.int32, (CHUNK, CHUNK), 0)
    ss = lax.broadcasted_iota(jnp.int32, (CHUNK, CHUNK), 1)
    causal = tt > ss
    diag = tt == ss
    level_masks = [jnp.logical_and(((tt ^ ss) >> li) == 1, causal) for li in range(len(LEVELS))]

    for c in range(tile // CHUNK):
        r0 = c * CHUNK
        q = _silu(zin_s[r0:r0 + CHUNK, o0:o0 + hgrn_width])
        z = zin_s[r0:r0 + CHUNK, o0 + hgrn_width:o0 + 2 * hgrn_width]
        v = zin_s[r0:r0 + CHUNK, o0 + 2 * hgrn_width:o0 + 3 * hgrn_width]
        og = zin_s[r0:r0 + CHUNK, o0 + 3 * hgrn_width:o0 + 4 * hgrn_width]
        log_sig = jnp.minimum(z, 0.0) - jnp.log1p(jnp.exp(-jnp.abs(z)))
        t2 = log_1m + log_sig
        log_f = jnp.maximum(log_lb, t2) + jnp.log1p(jnp.exp(-jnp.abs(log_lb - t2)))
        k = (1.0 - lb) * _sigmoid(-z)

        hi = log_f.astype(bf16)
        lo = (log_f - hi.astype(f32)).astype(bf16)
        ex = jnp.dot(z_ref[...], jnp.concatenate([hi, lo], axis=0), preferred_element_type=f32)

        for hd in range(heads):
            ls = slice(hd * HEAD_DIM, (hd + 1) * HEAD_DIM)
            qh, kh = q[:, ls], k[:, ls]
            qb, kb, vb = qh.astype(bf16), kh.astype(bf16), v[:, ls].astype(bf16)
            a = jnp.where(diag, lax.dot_general(qb, kb, _NT, preferred_element_type=f32), 0.0)
            for li in range(len(LEVELS)):
                w = jnp.exp(ex[(2 + li) * CHUNK:(3 + li) * CHUNK, ls]).astype(bf16)
                p = lax.dot_general(qb * w, kb * w, _NT, preferred_element_type=f32)
                a = jnp.where(level_masks[li], p, a)
            b_cum = ex[0:CHUNK, ls]
            st = st_s[hd]
            q_in = (qh * jnp.exp(b_cum)).astype(bf16)
            o = (jnp.dot(a.astype(bf16), vb, preferred_element_type=f32)
                 + lax.dot_general(q_in, st.astype(bf16), _NT, preferred_element_type=f32))
            k_out = (kh * jnp.exp(ex[CHUNK:2 * CHUNK, ls])).astype(bf16)
            st_s[hd] = (st * jnp.exp(b_cum[CHUNK - 1:CHUNK, :])
                        + lax.dot_general(vb, k_out, _TN, preferred_element_type=f32))
            o = _rms(o, ng_ref[...]) * _silu(og[:, ls])
            mix_s[r0:r0 + CHUNK, conv_width + hd * HEAD_DIM:conv_width + (hd + 1) * HEAD_DIM] = o.astype(bf16)

    mixed = jnp.dot(mix_s[...], wout_ref[...], preferred_element_type=f32)
    o_ref[0] = x + gt1 * _rms(mixed, gpost_ref[...])


def _ffn_kernel(x_ref, mod_ref, gpre_ref, gpost_ref, w1_ref, w2_ref, o_ref):
    f32, bf16 = jnp.float32, jnp.bfloat16
    d_ff = w2_ref.shape[0]
    x = x_ref[0]
    mod = mod_ref[0]
    sh2, sc2, gt2 = mod[3:4], mod[4:5], mod[5:6]
    h = (_rms(x, gpre_ref[...]) * (1.0 + sc2) + sh2).astype(bf16)
    gu = jnp.dot(h, w1_ref[...], preferred_element_type=f32)
    a = (_silu(gu[:, 0:d_ff]) * gu[:, d_ff:2 * d_ff]).astype(bf16)
    y = jnp.dot(a, w2_ref[...], preferred_element_type=f32)
    o_ref[0] = x + gt2 * _rms(y, gpost_ref[...])


def _const_spec(shape):
    return pl.BlockSpec(shape, lambda *_: (0,) * len(shape))


def _modulation(c, w_ada, b_ada):
    depth, d, n = w_ada.shape
    b = c.shape[0]
    tn = 1024
    return pl.pallas_call(
        _mod_kernel,
        grid=(depth, n // tn),
        in_specs=[pl.BlockSpec((b, d), lambda l, j: (0, 0)),
                  pl.BlockSpec((1, d, tn), lambda l, j: (l, 0, j)),
                  pl.BlockSpec((1, 1, tn), lambda l, j: (l, 0, j))],
        out_specs=pl.BlockSpec((1, b, tn), lambda l, j: (l, 0, j)),
        out_shape=jax.ShapeDtypeStruct((depth, b, n), jnp.float32),
        compiler_params=pltpu.CompilerParams(dimension_semantics=("arbitrary", "arbitrary"),
                                             vmem_limit_bytes=VMEM_LIMIT_BYTES),
        name="adaln_mod",
    )(c, w_ada, b_ada.reshape(depth, 1, n))


def _mixer(x, mod, g_pre, g_post, w_in, conv_w, conv_b, ln_g, ln_b, lower_bounds, norm_g, w_out, zmat,
           *, layer, tile):
    b, s, d = x.shape
    in_width = w_in.shape[1]
    conv_width = conv_w.shape[1]
    depth, hgrn_width = lower_bounds.shape
    heads = hgrn_width // HEAD_DIM
    mix_width = w_out.shape[0]
    kern = functools.partial(_mixer_kernel, layer=layer, depth=depth)
    return pl.pallas_call(
        kern,
        grid=(b, s // tile),
        in_specs=[pl.BlockSpec((1, tile, d), lambda i, t: (i, t, 0)),
                  pl.BlockSpec((1, N_MOD, d), lambda i, t: (i, 0, 0)),
                  _const_spec((1, d)), _const_spec((1, d)),
                  _const_spec((d, in_width)),
                  _const_spec((CONV_KERNEL, conv_width)), _const_spec((1, conv_width)),
                  _const_spec((1, conv_width)), _const_spec((1, conv_width)),
                  _const_spec((depth, hgrn_width)), _const_spec((1, HEAD_DIM)),
                  _const_spec((mix_width, d)),
                  _const_spec(zmat.shape)],
        out_specs=pl.BlockSpec((1, tile, d), lambda i, t: (i, t, 0)),
        out_shape=jax.ShapeDtypeStruct((b, s, d), jnp.float32),
        scratch_shapes=[pltpu.VMEM((tile, in_width), jnp.float32),
                        pltpu.VMEM((tile + CONV_HALO, conv_width), jnp.float32),
                        pltpu.VMEM((heads, HEAD_DIM, HEAD_DIM), jnp.float32),
                        pltpu.VMEM((tile, mix_width), jnp.bfloat16)],
        compiler_params=pltpu.CompilerParams(dimension_semantics=("arbitrary", "arbitrary"),
                                             vmem_limit_bytes=VMEM_LIMIT_BYTES),
        name=f"mixer_l{layer}",
    )(x, mod, g_pre, g_post, w_in, conv_w, conv_b, ln_g, ln_b, lower_bounds, norm_g, w_out, zmat)


def _ffn(x, mod, g_pre, g_post, w1, w2, *, layer, tile):
    b, s, d = x.shape
    return pl.pallas_call(
        _ffn_kernel,
        grid=(b, s // tile),
        in_specs=[pl.BlockSpec((1, tile, d), lambda i, t: (i, t, 0)),
                  pl.BlockSpec((1, N_MOD, d), lambda i, t: (i, 0, 0)),
                  _const_spec((1, d)), _const_spec((1, d)),
                  _const_spec(w1.shape), _const_spec(w2.shape)],
        out_specs=pl.BlockSpec((1, tile, d), lambda i, t: (i, t, 0)),
        out_shape=jax.ShapeDtypeStruct((b, s, d), jnp.float32),
        compiler_params=pltpu.CompilerParams(dimension_semantics=("arbitrary", "arbitrary"),
                                             vmem_limit_bytes=VMEM_LIMIT_BYTES),
        name=f"ffn_l{layer}",
    )(x, mod, g_pre, g_post, w1, w2)


def kernel(x, c, w_ada, b_ada, g_pre_mix, g_post_mix, w_in, conv_w, conv_b, conv_ln_g, conv_ln_b,
           hgrn_lower_bounds, hgrn_norm_g, w_out, g_pre_ffn, g_post_ffn, w_ffn_in, w_ffn_out):
    depth = w_ada.shape[0]
    b, s, d = x.shape
    bf16 = jnp.bfloat16
    mod = _modulation(c, w_ada, b_ada).reshape(depth, b, N_MOD, d)
    zmat = jnp.asarray(_decay_matrix(), dtype=bf16)
    tile = min(256, s)
    for l in range(depth):
        row = lambda p: p[l][None, :]
        x = _mixer(x, mod[l], row(g_pre_mix), row(g_post_mix), w_in[l].astype(bf16), conv_w[l],
                   row(conv_b), row(conv_ln_g), row(conv_ln_b), hgrn_lower_bounds, row(hgrn_norm_g),
                   w_out[l].astype(bf16), zmat, layer=l, tile=tile)
        x = _ffn(x, mod[l], row(g_pre_ffn), row(g_post_ffn), w_ffn_in[l].astype(bf16),
                 w_ffn_out[l].astype(bf16), layer=l, tile=tile)
    return x
```

```python
import functools

import numpy as np
import jax
import jax.numpy as jnp
from jax import lax
from jax.experimental import pallas as pl
from jax.experimental.pallas import tpu as pltpu

EPS = 1e-6
CONV_KERNEL = 31
CONV_HALO = 32
CONV_ROWS = 32
SUBLANES = 8
HEAD_DIM = 128
CHUNK = 64
MXU_COLS = 256
LEVELS = (1, 2, 4, 8, 16, 32)
N_MOD = 6
VMEM_LIMIT_BYTES = 56 * 1024 * 1024

_NT = (((1,), (1,)), ((), ()))
_TN = (((0,), (0,)), ((), ()))


def _decay_matrix():
    t = np.arange(CHUNK)[:, None]
    j = np.arange(CHUNK)[None, :]
    blocks = [(j <= t), (j > t)]
    for n in LEVELS:
        p = (t // (2 * n)) * (2 * n) + n - 1
        upper = (t % (2 * n)) >= n
        blocks.append(np.where(upper, (j > p) & (j <= t), (j > t) & (j <= p)))
    z = np.concatenate(blocks, axis=0).astype(np.float32)
    return np.concatenate([z, z], axis=1)


def _sigmoid(x):
    return 1.0 / (1.0 + jnp.exp(-x))


def _silu(x):
    return x * _sigmoid(x)


def _rms(x, g):
    return x * lax.rsqrt(jnp.mean(x * x, axis=-1, keepdims=True) + EPS) * g


def _mod_kernel(c_ref, w_ref, b_ref, o_ref):
    c = c_ref[...]
    o_ref[0] = jnp.dot(_silu(c), w_ref[0], preferred_element_type=jnp.float32) + b_ref[0]


def _own_block(fn, index):
    pl.when(pl.program_id(0) >= -index)(fn)


def _lockstep(streams, filler, n_filler):
    active = list(streams)
    tick = 0
    while active or tick < n_filler:
        for s in list(active):
            if next(s, active) is active:
                active.remove(s)
        if tick < n_filler:
            next(filler, None)
        tick += 1


def _mixer_stage1(x_ref, mod_ref, gpre_ref, win_ref, zin_w, x_w, h_s):
    x = x_ref[0]
    mod = mod_ref[0]
    sh1, sc1 = mod[0:1], mod[1:2]
    h_s[...] = (_rms(x, gpre_ref[...]) * (1.0 + sc1) + sh1).astype(jnp.bfloat16)
    x_w[...] = x
    yield
    for j in range(win_ref.shape[1] // MXU_COLS):
        cs = slice(j * MXU_COLS, (j + 1) * MXU_COLS)
        zin_w[:, cs] = jnp.dot(h_s[...], win_ref[:, cs], preferred_element_type=jnp.float32)
        yield


def _mixer_stage2(zin_s, x_s, mod_ref, gpost_ref, cb_ref, lng_ref, lnb_ref, lb_ref, ng_ref, wout_ref, z_ref,
                  o_ref, ubuf_s, ush_s, cwb_s, st_s, mix_s, q_s, k_s, ex_s, filler, *, layer, depth):
    tile = x_s.shape[0]
    cw = CONV_HALO
    conv_width = cb_ref.shape[1]
    hgrn_width = lb_ref.shape[1]
    heads = hgrn_width // HEAD_DIM
    n_chunks = tile // CHUNK
    n_slots = tile // CONV_ROWS
    heads_per_slot = (n_chunks * heads) // n_slots
    n_pieces = zin_s.shape[1] // MXU_COLS
    o0 = 2 * conv_width
    f32, bf16 = jnp.float32, jnp.bfloat16

    def prepare():
        next(filler)
        u = zin_s[:, 0:conv_width] * _sigmoid(zin_s[:, conv_width:2 * conv_width])
        ubuf_s[cw:cw + tile, :] = u
        for r in range(1, SUBLANES):
            ush_s[r - 1, SUBLANES:cw + tile, :] = ubuf_s[SUBLANES - r:cw + tile - r, :]

        rows = [lb_ref[i:i + 1, :] for i in range(depth)]
        m = functools.reduce(jnp.maximum, rows)
        e = [jnp.exp(r_ - m) for r_ in rows]
        tot = functools.reduce(jnp.add, e)
        lb = functools.reduce(jnp.add, e[1:layer + 1], jnp.zeros_like(tot)) / tot
        log_lb = jnp.log(lb)
        log_1m = jnp.log(1.0 - lb)
        for c in range(n_chunks):
            r0 = c * CHUNK
            q_s[r0:r0 + CHUNK, :] = _silu(zin_s[r0:r0 + CHUNK, o0:o0 + hgrn_width])
            z = zin_s[r0:r0 + CHUNK, o0 + hgrn_width:o0 + 2 * hgrn_width]
            log_sig = jnp.minimum(z, 0.0) - jnp.log(1.0 + jnp.exp(-jnp.abs(z)))
            t2 = log_1m + log_sig
            log_f = jnp.maximum(log_lb, t2) + jnp.log(1.0 + jnp.exp(-jnp.abs(log_lb - t2)))
            k_s[r0:r0 + CHUNK, :] = (1.0 - lb) * _sigmoid(-z)
            hi = log_f.astype(bf16)
            lo = (log_f - hi.astype(f32)).astype(bf16)
            ex_s[c] = jnp.dot(z_ref[...], jnp.concatenate([hi, lo], axis=0), preferred_element_type=f32)

    def conv_block(r):
        rb = CONV_ROWS
        acc = jnp.broadcast_to(cb_ref[...], (rb, conv_width)).reshape(rb // SUBLANES, SUBLANES, conv_width)
        for j in range(CONV_KERNEL):
            delay = CONV_KERNEL - 1 - j
            start = r * rb + cw - (delay // SUBLANES) * SUBLANES
            res = delay % SUBLANES
            src = ubuf_s[start:start + rb, :] if res == 0 else ush_s[res - 1, start:start + rb, :]
            acc = acc + cwb_s[j][None] * src.reshape(rb // SUBLANES, SUBLANES, conv_width)
        acc = acc.reshape(rb, conv_width)
        mu = jnp.mean(acc, axis=-1, keepdims=True)
        yield
        d = acc - mu
        var = jnp.mean(d * d, axis=-1, keepdims=True)
        yield
        y = d * lax.rsqrt(var + EPS) * lng_ref[...] + lnb_ref[...]
        mix_s[r * rb:(r + 1) * rb, 0:conv_width] = _silu(y).astype(bf16)

    def head(c, hd):
        r0 = c * CHUNK
        rs = slice(r0, r0 + CHUNK)
        ls = slice(hd * HEAD_DIM, (hd + 1) * HEAD_DIM)
        tt = lax.broadcasted_iota(jnp.int32, (CHUNK, CHUNK), 0)
        ss = lax.broadcasted_iota(jnp.int32, (CHUNK, CHUNK), 1)
        qh, kh = q_s[rs, ls], k_s[rs, ls]
        vcol = o0 + 2 * hgrn_width + hd * HEAD_DIM
        qb, kb, vb = qh.astype(bf16), kh.astype(bf16), zin_s[rs, vcol:vcol + HEAD_DIM].astype(bf16)
        prods = [lax.dot_general(qb, kb, _NT, preferred_element_type=f32)]
        for li in range(len(LEVELS)):
            w = jnp.exp(ex_s[c, (2 + li) * CHUNK:(3 + li) * CHUNK, ls]).astype(bf16)
            prods.append(lax.dot_general(qb * w, kb * w, _NT, preferred_element_type=f32))
        b_cum = ex_s[c, 0:CHUNK, ls]
        q_in = (qh * jnp.exp(b_cum)).astype(bf16)
        k_out = (kh * jnp.exp(ex_s[c, CHUNK:2 * CHUNK, ls])).astype(bf16)
        d_st = lax.dot_general(vb, k_out, _TN, preferred_element_type=f32)
        yield
        a = jnp.where(tt == ss, prods[0], 0.0)
        for li in range(len(LEVELS)):
            a = jnp.where(jnp.logical_and(((tt ^ ss) >> li) == 1, tt > ss), prods[li + 1], a)
        st = st_s[hd]
        o = (jnp.dot(a.astype(bf16), vb, preferred_element_type=f32)
             + lax.dot_general(q_in, st.astype(bf16), _NT, preferred_element_type=f32))
        st_s[hd] = st * jnp.exp(b_cum[CHUNK - 1:CHUNK, :]) + d_st
        yield
        ms = jnp.mean(o * o, axis=-1, keepdims=True)
        gcol = o0 + 3 * hgrn_width + hd * HEAD_DIM
        gate = _silu(zin_s[rs, gcol:gcol + HEAD_DIM])
        yield
        o = o * lax.rsqrt(ms + EPS) * ng_ref[...] * gate
        mix_s[rs, conv_width + hd * HEAD_DIM:conv_width + (hd + 1) * HEAD_DIM] = o.astype(bf16)

    def slot(s):
        def body():
            first = s * heads_per_slot
            streams = [head(i // heads, i % heads) for i in range(first, first + heads_per_slot)]
            streams.append(conv_block(s))
            _lockstep(streams, filler, n_pieces * (s + 1) // n_slots - n_pieces * s // n_slots)
        return body

    def finish():
        ubuf_s[0:cw, :] = ubuf_s[tile:tile + cw, :]
        for _ in filler:
            pass
        mixed = jnp.dot(mix_s[...], wout_ref[...], preferred_element_type=f32)
        gt1 = mod_ref[0][2:3]
        o_ref[0] = x_s[...] + gt1 * _rms(mixed, gpost_ref[...])

    prepare()
    for s in range(n_slots):
        slot(s)()
    finish()


def _mixer_kernel(xn_ref, modn_ref, modc_ref, gpre_ref, gpost_ref, win_ref, cw_ref, cb_ref, lng_ref, lnb_ref,
                  lb_ref, ng_ref, wout_ref, z_ref, o_ref, zin_a, zin_b, x_a, x_b, ubuf_s, ush_s, cwb_s, st_s,
                  mix_s, h_s, q_s, k_s, ex_s, *, layer, depth, tiles_per_row):
    g = pl.program_id(0)
    f32 = jnp.float32

    @pl.when(g == 0)
    def _():
        zin_b[...] = jnp.zeros(zin_b.shape, f32)
        x_b[...] = jnp.zeros(x_b.shape, f32)
        for j in range(CONV_KERNEL):
            cwb_s[j] = jnp.broadcast_to(cw_ref[j:j + 1, :], (SUBLANES, cw_ref.shape[1]))

    @pl.when(jnp.maximum(g - 1, 0) % tiles_per_row == 0)
    def _():
        ubuf_s[0:CONV_HALO, :] = jnp.zeros((CONV_HALO, ubuf_s.shape[1]), f32)
        st_s[...] = jnp.zeros(st_s.shape, f32)

    def step(zin_w, x_w, zin_r, x_r):
        proj = _mixer_stage1(xn_ref, modn_ref, gpre_ref, win_ref, zin_w, x_w, h_s)
        _mixer_stage2(zin_r, x_r, modc_ref, gpost_ref, cb_ref, lng_ref, lnb_ref, lb_ref, ng_ref, wout_ref,
                      z_ref, o_ref, ubuf_s, ush_s, cwb_s, st_s, mix_s, q_s, k_s, ex_s, proj,
                      layer=layer, depth=depth)

    @pl.when(g % 2 == 0)
    def _():
        step(zin_a, x_a, zin_b, x_b)

    @pl.when(g % 2 == 1)
    def _():
        step(zin_b, x_b, zin_a, x_a)


def _ffn_kernel(x_ref, mod_ref, gpre_ref, gpost_ref, w1_ref, w2_ref, o_ref):
    f32, bf16 = jnp.float32, jnp.bfloat16
    d_ff = w2_ref.shape[0]
    x = x_ref[0]
    mod = mod_ref[0]
    sh2, sc2, gt2 = mod[3:4], mod[4:5], mod[5:6]
    h = (_rms(x, gpre_ref[...]) * (1.0 + sc2) + sh2).astype(bf16)
    gu = jnp.dot(h, w1_ref[...], preferred_element_type=f32)
    a = (_silu(gu[:, 0:d_ff]) * gu[:, d_ff:2 * d_ff]).astype(bf16)
    y = jnp.dot(a, w2_ref[...], preferred_element_type=f32)
    o_ref[0] = x + gt2 * _rms(y, gpost_ref[...])


def _const_spec(shape):
    return pl.BlockSpec(shape, lambda *_: (0,) * len(shape))


def _modulation(c, w_ada, b_ada):
    depth, d, n = w_ada.shape
    b = c.shape[0]
    tn = 1024
    return pl.pallas_call(
        _mod_kernel,
        grid=(depth, n // tn),
        in_specs=[pl.BlockSpec((b, d), lambda l, j: (0, 0)),
                  pl.BlockSpec((1, d, tn), lambda l, j: (l, 0, j)),
                  pl.BlockSpec((1, 1, tn), lambda l, j: (l, 0, j))],
        out_specs=pl.BlockSpec((1, b, tn), lambda l, j: (l, 0, j)),
        out_shape=jax.ShapeDtypeStruct((depth, b, n), jnp.float32),
        compiler_params=pltpu.CompilerParams(dimension_semantics=("arbitrary", "arbitrary"),
                                             vmem_limit_bytes=VMEM_LIMIT_BYTES),
        name="adaln_mod",
    )(c, w_ada, b_ada.reshape(depth, 1, n))


def _mixer(x, mod, g_pre, g_post, w_in, conv_w, conv_b, ln_g, ln_b, lower_bounds, norm_g, w_out, zmat,
           *, layer, tile):
    b, s, d = x.shape
    in_width = w_in.shape[1]
    conv_width = conv_w.shape[1]
    depth, hgrn_width = lower_bounds.shape
    heads = hgrn_width // HEAD_DIM
    mix_width = w_out.shape[0]
    tiles_per_row = s // tile
    n_tiles = b * tiles_per_row
    nxt = lambda g: jnp.minimum(g, n_tiles - 1)
    cur = lambda g: jnp.maximum(g - 1, 0)
    kern = functools.partial(_mixer_kernel, layer=layer, depth=depth, tiles_per_row=tiles_per_row)
    out = pl.pallas_call(
        kern,
        grid=(n_tiles + 1,),
        in_specs=[pl.BlockSpec((1, tile, d), lambda g: (nxt(g), 0, 0)),
                  pl.BlockSpec((1, N_MOD, d), lambda g: (nxt(g) // tiles_per_row, 0, 0)),
                  pl.BlockSpec((1, N_MOD, d), lambda g: (cur(g) // tiles_per_row, 0, 0)),
                  _const_spec((1, d)), _const_spec((1, d)),
                  _const_spec((d, in_width)),
                  _const_spec((CONV_KERNEL, conv_width)), _const_spec((1, conv_width)),
                  _const_spec((1, conv_width)), _const_spec((1, conv_width)),
                  _const_spec((depth, hgrn_width)), _const_spec((1, HEAD_DIM)),
                  _const_spec((mix_width, d)),
                  _const_spec(zmat.shape)],
        out_specs=pl.BlockSpec((1, tile, d), lambda g: (cur(g), 0, 0)),
        out_shape=jax.ShapeDtypeStruct((n_tiles, tile, d), jnp.float32),
        scratch_shapes=[pltpu.VMEM((tile, in_width), jnp.float32),
                        pltpu.VMEM((tile, in_width), jnp.float32),
                        pltpu.VMEM((tile, d), jnp.float32),
                        pltpu.VMEM((tile, d), jnp.float32),
                        pltpu.VMEM((tile + CONV_HALO, conv_width), jnp.float32),
                        pltpu.VMEM((SUBLANES - 1, tile + CONV_HALO, conv_width), jnp.float32),
                        pltpu.VMEM((CONV_KERNEL, SUBLANES, conv_width), jnp.float32),
                        pltpu.VMEM((heads, HEAD_DIM, HEAD_DIM), jnp.float32),
                        pltpu.VMEM((tile, mix_width), jnp.bfloat16),
                        pltpu.VMEM((tile, d), jnp.bfloat16),
                        pltpu.VMEM((tile, hgrn_width), jnp.float32),
                        pltpu.VMEM((tile, hgrn_width), jnp.float32),
                        pltpu.VMEM((tile // CHUNK, (2 + len(LEVELS)) * CHUNK, hgrn_width), jnp.float32)],
        compiler_params=pltpu.CompilerParams(dimension_semantics=("arbitrary",),
                                             vmem_limit_bytes=VMEM_LIMIT_BYTES),
        name=f"mixer_l{layer}",
    )(x.reshape(n_tiles, tile, d), mod, mod, g_pre, g_post, w_in, conv_w, conv_b, ln_g, ln_b, lower_bounds,
      norm_g, w_out, zmat)
    return out.reshape(b, s, d)


def _ffn(x, mod, g_pre, g_post, w1, w2, *, layer, tile):
    b, s, d = x.shape
    return pl.pallas_call(
        _ffn_kernel,
        grid=(b, s // tile),
        in_specs=[pl.BlockSpec((1, tile, d), lambda i, t: (i, t, 0)),
                  pl.BlockSpec((1, N_MOD, d), lambda i, t: (i, 0, 0)),
                  _const_spec((1, d)), _const_spec((1, d)),
                  _const_spec(w1.shape), _const_spec(w2.shape)],
        out_specs=pl.BlockSpec((1, tile, d), lambda i, t: (i, t, 0)),
        out_shape=jax.ShapeDtypeStruct((b, s, d), jnp.float32),
        compiler_params=pltpu.CompilerParams(dimension_semantics=("arbitrary", "arbitrary"),
                                             vmem_limit_bytes=VMEM_LIMIT_BYTES),
        name=f"ffn_l{layer}",
    )(x, mod, g_pre, g_post, w1, w2)


def kernel(x, c, w_ada, b_ada, g_pre_mix, g_post_mix, w_in, conv_w, conv_b, conv_ln_g, conv_ln_b,
           hgrn_lower_bounds, hgrn_norm_g, w_out, g_pre_ffn, g_post_ffn, w_ffn_in, w_ffn_out):
    depth = w_ada.shape[0]
    b, s, d = x.shape
    bf16 = jnp.bfloat16
    mod = _modulation(c, w_ada, b_ada).reshape(depth, b, N_MOD, d)
    zmat = jnp.asarray(_decay_matrix(), dtype=bf16)
    tile = min(256, s)
    for l in range(depth):
        row = lambda p: p[l][None, :]
        x = _mixer(x, mod[l], row(g_pre_mix), row(g_post_mix), w_in[l].astype(bf16), conv_w[l],
                   row(conv_b), row(conv_ln_g), row(conv_ln_b), hgrn_lower_bounds, row(hgrn_norm_g),
                   w_out[l].astype(bf16), zmat, layer=l, tile=tile)
        x = _ffn(x, mod[l], row(g_pre_ffn), row(g_post_ffn), w_ffn_in[l].astype(bf16),
                 w_ffn_out[l].astype(bf16), layer=l, tile=tile)
    return x
```

```python
import functools

import numpy as np
import jax
import jax.numpy as jnp
from jax import lax
from jax.experimental import pallas as pl
from jax.experimental.pallas import tpu as pltpu

EPS = 1e-6
CONV_KERNEL = 31
CONV_HALO = 32
CONV_ROWS = 32
SUBLANES = 8
HEAD_DIM = 128
CHUNK = 64
MXU_COLS = 256
LEVELS = (32, 16, 8, 4, 2, 1)
SPLIT_BLOCK = 32
SPLIT_LEVELS = tuple(n for n in LEVELS if n >= SPLIT_BLOCK)
MAX_SPLIT_EXPONENT = 70.0
N_MOD = 6
VMEM_LIMIT_BYTES = 56 * 1024 * 1024

_NT = (((1,), (1,)), ((), ()))
_TN = (((0,), (0,)), ((), ()))


def _level_rows(n):
    t = np.arange(CHUNK)[:, None]
    j = np.arange(CHUNK)[None, :]
    p = (t // (2 * n)) * (2 * n) + n - 1
    upper = (t % (2 * n)) >= n
    return np.where(upper, (j > p) & (j <= t), (j > t) & (j <= p)).astype(np.float32)


def _decay_matrix():
    t = np.arange(CHUNK)[:, None]
    j = np.arange(CHUNK)[None, :]
    mid = (t // SPLIT_BLOCK) * SPLIT_BLOCK + SPLIT_BLOCK // 2 - 1
    rel = ((j > mid) & (j <= t)).astype(np.float32) - ((j > t) & (j <= mid)).astype(np.float32)
    blocks = [(j <= t).astype(np.float32), (j > t).astype(np.float32)]
    blocks += [_level_rows(n) for n in SPLIT_LEVELS] + [rel]
    blocks += [_level_rows(n) for n in LEVELS if n not in SPLIT_LEVELS]
    z = np.concatenate(blocks, axis=0)
    return np.concatenate([z, z], axis=1)


def _level_block(n):
    if n in SPLIT_LEVELS:
        return 2 + SPLIT_LEVELS.index(n)
    rest = [m for m in LEVELS if m not in SPLIT_LEVELS]
    return 3 + len(SPLIT_LEVELS) + rest.index(n)


_REL_BLOCK = 2 + len(SPLIT_LEVELS)
_SHORT_ROWS = (_REL_BLOCK + 1) * CHUNK
_ALL_ROWS = (3 + len(LEVELS)) * CHUNK


def _sigmoid(x):
    return 1.0 / (1.0 + jnp.exp(-x))


def _silu(x):
    return x * _sigmoid(x)


def _rms(x, g):
    return x * lax.rsqrt(jnp.mean(x * x, axis=-1, keepdims=True) + EPS) * g


def _mod_kernel(c_ref, w_ref, b_ref, o_ref):
    c = c_ref[...]
    o_ref[0] = jnp.dot(_silu(c), w_ref[0], preferred_element_type=jnp.float32) + b_ref[0]


def _lockstep(streams, filler, n_filler):
    active = list(streams)
    tick = 0
    while active or tick < n_filler:
        for s in list(active):
            if next(s, active) is active:
                active.remove(s)
        if tick < n_filler:
            next(filler, None)
        tick += 1


def _projection_input(x_ref, mod_ref, gpre_ref, x_w, h_s):
    x = x_ref[0]
    mod = mod_ref[0]
    sh1, sc1 = mod[0:1], mod[1:2]
    h_s[...] = (_rms(x, gpre_ref[...]) * (1.0 + sc1) + sh1).astype(jnp.bfloat16)
    x_w[...] = x


def _projection_pieces(win_ref, zin_w, h_s):
    for j in range(win_ref.shape[1] // MXU_COLS):
        cs = slice(j * MXU_COLS, (j + 1) * MXU_COLS)
        zin_w[:, cs] = jnp.dot(h_s[...], win_ref[:, cs], preferred_element_type=jnp.float32)
        yield


def _mixer_stage2(zin_s, x_s, mod_ref, gpost_ref, cb_ref, lng_ref, lnb_ref, lb_ref, ng_ref, wout_ref, z_ref,
                  o_ref, ubuf_s, ush_s, cwb_s, st_s, mix_s, q_s, k_s, ex_s, hl_s, make_filler, *, layer, depth):
    tile = x_s.shape[0]
    cw = CONV_HALO
    conv_width = cb_ref.shape[1]
    hgrn_width = lb_ref.shape[1]
    heads = hgrn_width // HEAD_DIM
    n_chunks = tile // CHUNK
    n_slots = tile // CONV_ROWS
    heads_per_slot = (n_chunks * heads) // n_slots
    n_pieces = zin_s.shape[1] // MXU_COLS
    o0 = 2 * conv_width
    f32, bf16 = jnp.float32, jnp.bfloat16

    u = zin_s[:, 0:conv_width] * _sigmoid(zin_s[:, conv_width:2 * conv_width])
    ubuf_s[cw:cw + tile, :] = u
    for r in range(1, SUBLANES):
        ush_s[r - 1, SUBLANES:cw + tile, :] = ubuf_s[SUBLANES - r:cw + tile - r, :]

    rows = [lb_ref[i:i + 1, :] for i in range(depth)]
    m = functools.reduce(jnp.maximum, rows)
    e = [jnp.exp(r_ - m) for r_ in rows]
    tot = functools.reduce(jnp.add, e)
    lb = functools.reduce(jnp.add, e[1:layer + 1], jnp.zeros_like(tot)) / tot
    log_lb = jnp.log(lb)
    log_1m = jnp.log(1.0 - lb)
    span = jnp.zeros((1, hgrn_width), f32)
    for c in range(n_chunks):
        r0 = c * CHUNK
        q_s[r0:r0 + CHUNK, :] = _silu(zin_s[r0:r0 + CHUNK, o0:o0 + hgrn_width])
        z = zin_s[r0:r0 + CHUNK, o0 + hgrn_width:o0 + 2 * hgrn_width]
        log_sig = jnp.minimum(z, 0.0) - jnp.log(1.0 + jnp.exp(-jnp.abs(z)))
        t2 = log_1m + log_sig
        log_f = jnp.maximum(log_lb, t2) + jnp.log(1.0 + jnp.exp(-jnp.abs(log_lb - t2)))
        k_s[r0:r0 + CHUNK, :] = (1.0 - lb) * _sigmoid(-z)
        hi = log_f.astype(bf16)
        lo = (log_f - hi.astype(f32)).astype(bf16)
        hl = jnp.concatenate([hi, lo], axis=0)
        hl_s[c] = hl
        ex = jnp.dot(z_ref[0:_SHORT_ROWS, :], hl, preferred_element_type=f32)
        ex_s[c, 0:_SHORT_ROWS, :] = ex
        rel = jnp.abs(ex[_REL_BLOCK * CHUNK:(_REL_BLOCK + 1) * CHUNK, :])
        span = jnp.maximum(span, jnp.max(rel, axis=0, keepdims=True))
    short_path_ok = jnp.max(span) <= MAX_SPLIT_EXPONENT

    def conv_block(r):
        rb = CONV_ROWS
        acc = jnp.broadcast_to(cb_ref[...], (rb, conv_width)).reshape(rb // SUBLANES, SUBLANES, conv_width)
        for j in range(CONV_KERNEL):
            delay = CONV_KERNEL - 1 - j
            start = r * rb + cw - (delay // SUBLANES) * SUBLANES
            res = delay % SUBLANES
            src = ubuf_s[start:start + rb, :] if res == 0 else ush_s[res - 1, start:start + rb, :]
            acc = acc + cwb_s[j][None] * src.reshape(rb // SUBLANES, SUBLANES, conv_width)
        acc = acc.reshape(rb, conv_width)
        mu = jnp.mean(acc, axis=-1, keepdims=True)
        yield
        d = acc - mu
        var = jnp.mean(d * d, axis=-1, keepdims=True)
        yield
        y = d * lax.rsqrt(var + EPS) * lng_ref[...] + lnb_ref[...]
        mix_s[r * rb:(r + 1) * rb, 0:conv_width] = _silu(y).astype(bf16)

    def level_weight(c, n, ls):
        blk = _level_block(n)
        return jnp.exp(ex_s[c, blk * CHUNK:(blk + 1) * CHUNK, ls]).astype(bf16)

    def level_mask(n, tt, ss):
        return jnp.logical_and(((tt ^ ss) >> (n.bit_length() - 1)) == 1, tt > ss)

    def head(c, hd, short_path):
        r0 = c * CHUNK
        rs = slice(r0, r0 + CHUNK)
        ls = slice(hd * HEAD_DIM, (hd + 1) * HEAD_DIM)
        tt = lax.broadcasted_iota(jnp.int32, (CHUNK, CHUNK), 0)
        ss = lax.broadcasted_iota(jnp.int32, (CHUNK, CHUNK), 1)
        qh, kh = q_s[rs, ls], k_s[rs, ls]
        vcol = o0 + 2 * hgrn_width + hd * HEAD_DIM
        qb, kb, vb = qh.astype(bf16), kh.astype(bf16), zin_s[rs, vcol:vcol + HEAD_DIM].astype(bf16)
        b_cum = ex_s[c, 0:CHUNK, ls]
        levels = SPLIT_LEVELS if short_path else LEVELS
        if short_path:
            rel = ex_s[c, _REL_BLOCK * CHUNK:(_REL_BLOCK + 1) * CHUNK, ls]
            prods = [lax.dot_general((qh * jnp.exp(rel)).astype(bf16), (kh * jnp.exp(-rel)).astype(bf16), _NT,
                                     preferred_element_type=f32)]
        else:
            prods = [lax.dot_general(qb, kb, _NT, preferred_element_type=f32)]
        for n in levels:
            w = level_weight(c, n, ls)
            prods.append(lax.dot_general(qb * w, kb * w, _NT, preferred_element_type=f32))
        q_in = (qh * jnp.exp(b_cum)).astype(bf16)
        k_out = (kh * jnp.exp(ex_s[c, CHUNK:2 * CHUNK, ls])).astype(bf16)
        d_st = lax.dot_general(vb, k_out, _TN, preferred_element_type=f32)
        yield
        if short_path:
            inside = jnp.logical_and(tt // SPLIT_BLOCK == ss // SPLIT_BLOCK, tt >= ss)
        else:
            inside = tt == ss
        a = jnp.where(inside, prods[0], 0.0)
        for n, p in zip(levels, prods[1:]):
            a = jnp.where(level_mask(n, tt, ss), p, a)
        st = st_s[hd]
        o = (jnp.dot(a.astype(bf16), vb, preferred_element_type=f32)
             + lax.dot_general(q_in, st.astype(bf16), _NT, preferred_element_type=f32))
        st_s[hd] = st * jnp.exp(b_cum[CHUNK - 1:CHUNK, :]) + d_st
        yield
        ms = jnp.mean(o * o, axis=-1, keepdims=True)
        gcol = o0 + 3 * hgrn_width + hd * HEAD_DIM
        gate = _silu(zin_s[rs, gcol:gcol + HEAD_DIM])
        yield
        o = o * lax.rsqrt(ms + EPS) * ng_ref[...] * gate
        mix_s[rs, conv_width + hd * HEAD_DIM:conv_width + (hd + 1) * HEAD_DIM] = o.astype(bf16)

    def slots(short_path):
        filler = make_filler()
        if not short_path:
            for c in range(n_chunks):
                ex_s[c, _SHORT_ROWS:, :] = jnp.dot(z_ref[_SHORT_ROWS:, :], hl_s[c], preferred_element_type=f32)
        for s in range(n_slots):
            first = s * heads_per_slot
            streams = [head(i // heads, i % heads, short_path) for i in range(first, first + heads_per_slot)]
            streams.append(conv_block(s))
            _lockstep(streams, filler, n_pieces * (s + 1) // n_slots - n_pieces * s // n_slots)
        for _ in filler:
            pass

    @pl.when(short_path_ok)
    def _():
        slots(True)

    @pl.when(jnp.logical_not(short_path_ok))
    def _():
        slots(False)

    ubuf_s[0:cw, :] = ubuf_s[tile:tile + cw, :]
    mixed = jnp.dot(mix_s[...], wout_ref[...], preferred_element_type=f32)
    gt1 = mod_ref[0][2:3]
    o_ref[0] = x_s[...] + gt1 * _rms(mixed, gpost_ref[...])


def _mixer_kernel(xn_ref, modn_ref, modc_ref, gpre_ref, gpost_ref, win_ref, cw_ref, cb_ref, lng_ref, lnb_ref,
                  lb_ref, ng_ref, wout_ref, z_ref, o_ref, zin_a, zin_b, x_a, x_b, ubuf_s, ush_s, cwb_s, st_s,
                  mix_s, h_s, q_s, k_s, ex_s, hl_s, *, layer, depth, tiles_per_row):
    g = pl.program_id(0)
    f32 = jnp.float32

    @pl.when(g == 0)
    def _():
        zin_b[...] = jnp.zeros(zin_b.shape, f32)
        x_b[...] = jnp.zeros(x_b.shape, f32)
        for j in range(CONV_KERNEL):
            cwb_s[j] = jnp.broadcast_to(cw_ref[j:j + 1, :], (SUBLANES, cw_ref.shape[1]))

    @pl.when(jnp.maximum(g - 1, 0) % tiles_per_row == 0)
    def _():
        ubuf_s[0:CONV_HALO, :] = jnp.zeros((CONV_HALO, ubuf_s.shape[1]), f32)
        st_s[...] = jnp.zeros(st_s.shape, f32)

    def step(zin_w, x_w, zin_r, x_r):
        _projection_input(xn_ref, modn_ref, gpre_ref, x_w, h_s)
        _mixer_stage2(zin_r, x_r, modc_ref, gpost_ref, cb_ref, lng_ref, lnb_ref, lb_ref, ng_ref, wout_ref,
                      z_ref, o_ref, ubuf_s, ush_s, cwb_s, st_s, mix_s, q_s, k_s, ex_s, hl_s,
                      functools.partial(_projection_pieces, win_ref, zin_w, h_s), layer=layer, depth=depth)

    @pl.when(g % 2 == 0)
    def _():
        step(zin_a, x_a, zin_b, x_b)

    @pl.when(g % 2 == 1)
    def _():
        step(zin_b, x_b, zin_a, x_a)


def _ffn_kernel(x_ref, mod_ref, gpre_ref, gpost_ref, w1_ref, w2_ref, o_ref):
    f32, bf16 = jnp.float32, jnp.bfloat16
    d_ff = w2_ref.shape[0]
    x = x_ref[0]
    mod = mod_ref[0]
    sh2, sc2, gt2 = mod[3:4], mod[4:5], mod[5:6]
    h = (_rms(x, gpre_ref[...]) * (1.0 + sc2) + sh2).astype(bf16)
    gu = jnp.dot(h, w1_ref[...], preferred_element_type=f32)
    a = (_silu(gu[:, 0:d_ff]) * gu[:, d_ff:2 * d_ff]).astype(bf16)
    y = jnp.dot(a, w2_ref[...], preferred_element_type=f32)
    o_ref[0] = x + gt2 * _rms(y, gpost_ref[...])


def _const_spec(shape):
    return pl.BlockSpec(shape, lambda *_: (0,) * len(shape))


def _modulation(c, w_ada, b_ada):
    depth, d, n = w_ada.shape
    b = c.shape[0]
    tn = 1024
    return pl.pallas_call(
        _mod_kernel,
        grid=(depth, n // tn),
        in_specs=[pl.BlockSpec((b, d), lambda l, j: (0, 0)),
                  pl.BlockSpec((1, d, tn), lambda l, j: (l, 0, j)),
                  pl.BlockSpec((1, 1, tn), lambda l, j: (l, 0, j))],
        out_specs=pl.BlockSpec((1, b, tn), lambda l, j: (l, 0, j)),
        out_shape=jax.ShapeDtypeStruct((depth, b, n), jnp.float32),
        compiler_params=pltpu.CompilerParams(dimension_semantics=("arbitrary", "arbitrary"),
                                             vmem_limit_bytes=VMEM_LIMIT_BYTES),
        name="adaln_mod",
    )(c, w_ada, b_ada.reshape(depth, 1, n))


def _mixer(x, mod, g_pre, g_post, w_in, conv_w, conv_b, ln_g, ln_b, lower_bounds, norm_g, w_out, zmat,
           *, layer, tile):
    b, s, d = x.shape
    in_width = w_in.shape[1]
    conv_width = conv_w.shape[1]
    depth, hgrn_width = lower_bounds.shape
    heads = hgrn_width // HEAD_DIM
    mix_width = w_out.shape[0]
    tiles_per_row = s // tile
    n_tiles = b * tiles_per_row
    nxt = lambda g: jnp.minimum(g, n_tiles - 1)
    cur = lambda g: jnp.maximum(g - 1, 0)
    kern = functools.partial(_mixer_kernel, layer=layer, depth=depth, tiles_per_row=tiles_per_row)
    out = pl.pallas_call(
        kern,
        grid=(n_tiles + 1,),
        in_specs=[pl.BlockSpec((1, tile, d), lambda g: (nxt(g), 0, 0)),
                  pl.BlockSpec((1, N_MOD, d), lambda g: (nxt(g) // tiles_per_row, 0, 0)),
                  pl.BlockSpec((1, N_MOD, d), lambda g: (cur(g) // tiles_per_row, 0, 0)),
                  _const_spec((1, d)), _const_spec((1, d)),
                  _const_spec((d, in_width)),
                  _const_spec((CONV_KERNEL, conv_width)), _const_spec((1, conv_width)),
                  _const_spec((1, conv_width)), _const_spec((1, conv_width)),
                  _const_spec((depth, hgrn_width)), _const_spec((1, HEAD_DIM)),
                  _const_spec((mix_width, d)),
                  _const_spec(zmat.shape)],
        out_specs=pl.BlockSpec((1, tile, d), lambda g: (cur(g), 0, 0)),
        out_shape=jax.ShapeDtypeStruct((n_tiles, tile, d), jnp.float32),
        scratch_shapes=[pltpu.VMEM((tile, in_width), jnp.float32),
                        pltpu.VMEM((tile, in_width), jnp.float32),
                        pltpu.VMEM((tile, d), jnp.float32),
                        pltpu.VMEM((tile, d), jnp.float32),
                        pltpu.VMEM((tile + CONV_HALO, conv_width), jnp.float32),
                        pltpu.VMEM((SUBLANES - 1, tile + CONV_HALO, conv_width), jnp.float32),
                        pltpu.VMEM((CONV_KERNEL, SUBLANES, conv_width), jnp.float32),
                        pltpu.VMEM((heads, HEAD_DIM, HEAD_DIM), jnp.float32),
                        pltpu.VMEM((tile, mix_width), jnp.bfloat16),
                        pltpu.VMEM((tile, d), jnp.bfloat16),
                        pltpu.VMEM((tile, hgrn_width), jnp.float32),
                        pltpu.VMEM((tile, hgrn_width), jnp.float32),
                        pltpu.VMEM((tile // CHUNK, _ALL_ROWS, hgrn_width), jnp.float32),
                        pltpu.VMEM((tile // CHUNK, 2 * CHUNK, hgrn_width), jnp.bfloat16)],
        compiler_params=pltpu.CompilerParams(dimension_semantics=("arbitrary",),
                                             vmem_limit_bytes=VMEM_LIMIT_BYTES),
        name=f"mixer_l{layer}",
    )(x.reshape(n_tiles, tile, d), mod, mod, g_pre, g_post, w_in, conv_w, conv_b, ln_g, ln_b, lower_bounds,
      norm_g, w_out, zmat)
    return out.reshape(b, s, d)


def _ffn(x, mod, g_pre, g_post, w1, w2, *, layer, tile):
    b, s, d = x.shape
    return pl.pallas_call(
        _ffn_kernel,
        grid=(b, s // tile),
        in_specs=[pl.BlockSpec((1, tile, d), lambda i, t: (i, t, 0)),
                  pl.BlockSpec((1, N_MOD, d), lambda i, t: (i, 0, 0)),
                  _const_spec((1, d)), _const_spec((1, d)),
                  _const_spec(w1.shape), _const_spec(w2.shape)],
        out_specs=pl.BlockSpec((1, tile, d), lambda i, t: (i, t, 0)),
        out_shape=jax.ShapeDtypeStruct((b, s, d), jnp.float32),
        compiler_params=pltpu.CompilerParams(dimension_semantics=("arbitrary", "arbitrary"),
                                             vmem_limit_bytes=VMEM_LIMIT_BYTES),
        name=f"ffn_l{layer}",
    )(x, mod, g_pre, g_post, w1, w2)


def kernel(x, c, w_ada, b_ada, g_pre_mix, g_post_mix, w_in, conv_w, conv_b, conv_ln_g, conv_ln_b,
           hgrn_lower_bounds, hgrn_norm_g, w_out, g_pre_ffn, g_post_ffn, w_ffn_in, w_ffn_out):
    depth = w_ada.shape[0]
    b, s, d = x.shape
    bf16 = jnp.bfloat16
    mod = _modulation(c, w_ada, b_ada).reshape(depth, b, N_MOD, d)
    zmat = jnp.asarray(_decay_matrix(), dtype=bf16)
    tile = min(256, s)
    for l in range(depth):
        row = lambda p: p[l][None, :]
        x = _mixer(x, mod[l], row(g_pre_mix), row(g_post_mix), w_in[l].astype(bf16), conv_w[l],
                   row(conv_b), row(conv_ln_g), row(conv_ln_b), hgrn_lower_bounds, row(hgrn_norm_g),
                   w_out[l].astype(bf16), zmat, layer=l, tile=tile)
        x = _ffn(x, mod[l], row(g_pre_ffn), row(g_post_ffn), w_ffn_in[l].astype(bf16),
                 w_ffn_out[l].astype(bf16), layer=l, tile=tile)
    return x
```

```python
import functools

import numpy as np
import jax
import jax.numpy as jnp
from jax import lax
from jax.experimental import pallas as pl
from jax.experimental.pallas import tpu as pltpu

EPS = 1e-6
CONV_KERNEL = 31
CONV_HALO = 32
CONV_ROWS = 32
SUBLANES = 8
HEAD_DIM = 128
CHUNK = 64
MXU_COLS = 256
LEVELS = (32, 16, 8, 4, 2, 1)
SPLIT_BLOCK = 32
SPLIT_LEVELS = tuple(n for n in LEVELS if n >= SPLIT_BLOCK)
MAX_SPLIT_EXPONENT = 70.0
N_MOD = 6
VMEM_LIMIT_BYTES = 56 * 1024 * 1024

_NT = (((1,), (1,)), ((), ()))
_TN = (((0,), (0,)), ((), ()))


def _level_rows(n):
    t = np.arange(CHUNK)[:, None]
    j = np.arange(CHUNK)[None, :]
    p = (t // (2 * n)) * (2 * n) + n - 1
    upper = (t % (2 * n)) >= n
    return np.where(upper, (j > p) & (j <= t), (j > t) & (j <= p)).astype(np.float32)


def _decay_matrix():
    t = np.arange(CHUNK)[:, None]
    j = np.arange(CHUNK)[None, :]
    mid = (t // SPLIT_BLOCK) * SPLIT_BLOCK + SPLIT_BLOCK // 2 - 1
    rel = ((j > mid) & (j <= t)).astype(np.float32) - ((j > t) & (j <= mid)).astype(np.float32)
    blocks = [(j <= t).astype(np.float32), (j > t).astype(np.float32)]
    blocks += [_level_rows(n) for n in SPLIT_LEVELS] + [rel]
    blocks += [_level_rows(n) for n in LEVELS if n not in SPLIT_LEVELS]
    z = np.concatenate(blocks, axis=0)
    return np.concatenate([z, z], axis=1)


def _level_block(n):
    if n in SPLIT_LEVELS:
        return 2 + SPLIT_LEVELS.index(n)
    rest = [m for m in LEVELS if m not in SPLIT_LEVELS]
    return 3 + len(SPLIT_LEVELS) + rest.index(n)


_REL_BLOCK = 2 + len(SPLIT_LEVELS)
_SHORT_ROWS = (_REL_BLOCK + 1) * CHUNK
_ALL_ROWS = (3 + len(LEVELS)) * CHUNK


def _sigmoid(x):
    return 1.0 / (1.0 + jnp.exp(-x))


def _silu(x):
    return x * _sigmoid(x)


def _rms(x, g):
    return x * lax.rsqrt(jnp.mean(x * x, axis=-1, keepdims=True) + EPS) * g


def _mod_kernel(c_ref, w_ref, b_ref, o_ref):
    c = c_ref[...]
    o_ref[0] = jnp.dot(_silu(c), w_ref[0], preferred_element_type=jnp.float32) + b_ref[0]


def _lockstep(streams):
    active = list(streams)
    while active:
        for s in list(active):
            if next(s, active) is active:
                active.remove(s)


def _projection_input(x_ref, mod_ref, gpre_ref, x_w, h_s):
    x = x_ref[0]
    mod = mod_ref[0]
    sh1, sc1 = mod[0:1], mod[1:2]
    h_s[...] = (_rms(x, gpre_ref[...]) * (1.0 + sc1) + sh1).astype(jnp.bfloat16)
    x_w[...] = x


def _projection_pieces(win_ref, zin_w, h_s):
    for j in range(win_ref.shape[1] // MXU_COLS):
        cs = slice(j * MXU_COLS, (j + 1) * MXU_COLS)
        zin_w[:, cs] = jnp.dot(h_s[...], win_ref[:, cs], preferred_element_type=jnp.float32)
        yield


def _mixer_stage2(zin_s, x_s, mod_ref, gpost_ref, cb_ref, lng_ref, lnb_ref, lb_ref, ng_ref, wout_ref, z_ref,
                  o_ref, ubuf_s, ush_s, cwb_s, st_s, mix_s, q_s, k_s, ex_s, hl_s, make_filler, *, layer, depth):
    tile = x_s.shape[0]
    cw = CONV_HALO
    conv_width = cb_ref.shape[1]
    hgrn_width = lb_ref.shape[1]
    heads = hgrn_width // HEAD_DIM
    n_chunks = tile // CHUNK
    n_slots = tile // CONV_ROWS
    heads_per_slot = (n_chunks * heads) // n_slots
    o0 = 2 * conv_width
    f32, bf16 = jnp.float32, jnp.bfloat16

    filler = make_filler()
    u = zin_s[:, 0:conv_width] * _sigmoid(zin_s[:, conv_width:2 * conv_width])
    ubuf_s[cw:cw + tile, :] = u
    next(filler, None)
    for r in range(1, SUBLANES):
        ush_s[r - 1, SUBLANES:cw + tile, :] = ubuf_s[SUBLANES - r:cw + tile - r, :]
        next(filler, None)

    rows = [lb_ref[i:i + 1, :] for i in range(depth)]
    m = functools.reduce(jnp.maximum, rows)
    e = [jnp.exp(r_ - m) for r_ in rows]
    tot = functools.reduce(jnp.add, e)
    lb = functools.reduce(jnp.add, e[1:layer + 1], jnp.zeros_like(tot)) / tot
    log_lb = jnp.log(lb)
    log_1m = jnp.log(1.0 - lb)
    span = jnp.zeros((1, hgrn_width), f32)
    for c in range(n_chunks):
        r0 = c * CHUNK
        q_s[r0:r0 + CHUNK, :] = _silu(zin_s[r0:r0 + CHUNK, o0:o0 + hgrn_width])
        z = zin_s[r0:r0 + CHUNK, o0 + hgrn_width:o0 + 2 * hgrn_width]
        log_sig = jnp.minimum(z, 0.0) - jnp.log(1.0 + jnp.exp(-jnp.abs(z)))
        t2 = log_1m + log_sig
        log_f = jnp.maximum(log_lb, t2) + jnp.log(1.0 + jnp.exp(-jnp.abs(log_lb - t2)))
        k_s[r0:r0 + CHUNK, :] = (1.0 - lb) * _sigmoid(-z)
        hi = log_f.astype(bf16)
        lo = (log_f - hi.astype(f32)).astype(bf16)
        hl = jnp.concatenate([hi, lo], axis=0)
        hl_s[c] = hl
        ex = jnp.dot(z_ref[0:_SHORT_ROWS, :], hl, preferred_element_type=f32)
        ex_s[c, 0:_SHORT_ROWS, :] = ex
        rel = jnp.abs(ex[_REL_BLOCK * CHUNK:(_REL_BLOCK + 1) * CHUNK, :])
        span = jnp.maximum(span, jnp.max(rel, axis=0, keepdims=True))
        next(filler, None)
    for _ in filler:
        pass
    short_path_ok = jnp.max(span) <= MAX_SPLIT_EXPONENT

    def conv_block(r):
        rb = CONV_ROWS
        acc = jnp.broadcast_to(cb_ref[...], (rb, conv_width)).reshape(rb // SUBLANES, SUBLANES, conv_width)
        for j in range(CONV_KERNEL):
            delay = CONV_KERNEL - 1 - j
            start = r * rb + cw - (delay // SUBLANES) * SUBLANES
            res = delay % SUBLANES
            src = ubuf_s[start:start + rb, :] if res == 0 else ush_s[res - 1, start:start + rb, :]
            acc = acc + cwb_s[j][None] * src.reshape(rb // SUBLANES, SUBLANES, conv_width)
        acc = acc.reshape(rb, conv_width)
        mu = jnp.mean(acc, axis=-1, keepdims=True)
        yield
        d = acc - mu
        var = jnp.mean(d * d, axis=-1, keepdims=True)
        yield
        y = d * lax.rsqrt(var + EPS) * lng_ref[...] + lnb_ref[...]
        mix_s[r * rb:(r + 1) * rb, 0:conv_width] = _silu(y).astype(bf16)

    def level_weight(c, n, ls):
        blk = _level_block(n)
        return jnp.exp(ex_s[c, blk * CHUNK:(blk + 1) * CHUNK, ls]).astype(bf16)

    def level_mask(n, tt, ss):
        return jnp.logical_and(((tt ^ ss) >> (n.bit_length() - 1)) == 1, tt > ss)

    def head(c, hd, short_path):
        r0 = c * CHUNK
        rs = slice(r0, r0 + CHUNK)
        ls = slice(hd * HEAD_DIM, (hd + 1) * HEAD_DIM)
        tt = lax.broadcasted_iota(jnp.int32, (CHUNK, CHUNK), 0)
        ss = lax.broadcasted_iota(jnp.int32, (CHUNK, CHUNK), 1)
        qh, kh = q_s[rs, ls], k_s[rs, ls]
        vcol = o0 + 2 * hgrn_width + hd * HEAD_DIM
        qb, kb, vb = qh.astype(bf16), kh.astype(bf16), zin_s[rs, vcol:vcol + HEAD_DIM].astype(bf16)
        b_cum = ex_s[c, 0:CHUNK, ls]
        levels = SPLIT_LEVELS if short_path else LEVELS
        if short_path:
            rel = ex_s[c, _REL_BLOCK * CHUNK:(_REL_BLOCK + 1) * CHUNK, ls]
            prods = [lax.dot_general((qh * jnp.exp(rel)).astype(bf16), (kh * jnp.exp(-rel)).astype(bf16), _NT,
                                     preferred_element_type=f32)]
        else:
            prods = [lax.dot_general(qb, kb, _NT, preferred_element_type=f32)]
        for n in levels:
            w = level_weight(c, n, ls)
            prods.append(lax.dot_general(qb * w, kb * w, _NT, preferred_element_type=f32))
        q_in = (qh * jnp.exp(b_cum)).astype(bf16)
        k_out = (kh * jnp.exp(ex_s[c, CHUNK:2 * CHUNK, ls])).astype(bf16)
        d_st = lax.dot_general(vb, k_out, _TN, preferred_element_type=f32)
        yield
        if short_path:
            inside = jnp.logical_and(tt // SPLIT_BLOCK == ss // SPLIT_BLOCK, tt >= ss)
        else:
            inside = tt == ss
        a = jnp.where(inside, prods[0], 0.0)
        for n, p in zip(levels, prods[1:]):
            a = jnp.where(level_mask(n, tt, ss), p, a)
        st = st_s[hd]
        o = (jnp.dot(a.astype(bf16), vb, preferred_element_type=f32)
             + lax.dot_general(q_in, st.astype(bf16), _NT, preferred_element_type=f32))
        st_s[hd] = st * jnp.exp(b_cum[CHUNK - 1:CHUNK, :]) + d_st
        yield
        ms = jnp.mean(o * o, axis=-1, keepdims=True)
        gcol = o0 + 3 * hgrn_width + hd * HEAD_DIM
        gate = _silu(zin_s[rs, gcol:gcol + HEAD_DIM])
        yield
        o = o * lax.rsqrt(ms + EPS) * ng_ref[...] * gate
        mix_s[rs, conv_width + hd * HEAD_DIM:conv_width + (hd + 1) * HEAD_DIM] = o.astype(bf16)

    def slots(short_path):
        if not short_path:
            for c in range(n_chunks):
                ex_s[c, _SHORT_ROWS:, :] = jnp.dot(z_ref[_SHORT_ROWS:, :], hl_s[c], preferred_element_type=f32)
        for s in range(n_slots):
            first = s * heads_per_slot
            streams = [head(i // heads, i % heads, short_path) for i in range(first, first + heads_per_slot)]
            streams.append(conv_block(s))
            _lockstep(streams)

    @pl.when(short_path_ok)
    def _():
        slots(True)

    @pl.when(jnp.logical_not(short_path_ok))
    def _():
        slots(False)

    ubuf_s[0:cw, :] = ubuf_s[tile:tile + cw, :]
    mixed = jnp.dot(mix_s[...], wout_ref[...], preferred_element_type=f32)
    gt1 = mod_ref[0][2:3]
    o_ref[0] = x_s[...] + gt1 * _rms(mixed, gpost_ref[...])


def _mixer_kernel(xn_ref, modn_ref, modc_ref, gpre_ref, gpost_ref, win_ref, cw_ref, cb_ref, lng_ref, lnb_ref,
                  lb_ref, ng_ref, wout_ref, z_ref, o_ref, zin_a, zin_b, x_a, x_b, ubuf_s, ush_s, cwb_s, st_s,
                  mix_s, h_s, q_s, k_s, ex_s, hl_s, *, layer, depth, tiles_per_row):
    g = pl.program_id(0)
    f32 = jnp.float32

    @pl.when(g == 0)
    def _():
        zin_b[...] = jnp.zeros(zin_b.shape, f32)
        x_b[...] = jnp.zeros(x_b.shape, f32)
        for j in range(CONV_KERNEL):
            cwb_s[j] = jnp.broadcast_to(cw_ref[j:j + 1, :], (SUBLANES, cw_ref.shape[1]))

    @pl.when(jnp.maximum(g - 1, 0) % tiles_per_row == 0)
    def _():
        ubuf_s[0:CONV_HALO, :] = jnp.zeros((CONV_HALO, ubuf_s.shape[1]), f32)
        st_s[...] = jnp.zeros(st_s.shape, f32)

    def step(zin_w, x_w, zin_r, x_r):
        _projection_input(xn_ref, modn_ref, gpre_ref, x_w, h_s)
        _mixer_stage2(zin_r, x_r, modc_ref, gpost_ref, cb_ref, lng_ref, lnb_ref, lb_ref, ng_ref, wout_ref,
                      z_ref, o_ref, ubuf_s, ush_s, cwb_s, st_s, mix_s, q_s, k_s, ex_s, hl_s,
                      functools.partial(_projection_pieces, win_ref, zin_w, h_s), layer=layer, depth=depth)

    @pl.when(g % 2 == 0)
    def _():
        step(zin_a, x_a, zin_b, x_b)

    @pl.when(g % 2 == 1)
    def _():
        step(zin_b, x_b, zin_a, x_a)


def _ffn_kernel(x_ref, mod_ref, gpre_ref, gpost_ref, w1_ref, w2_ref, o_ref):
    f32 = jnp.float32
    d_ff = w2_ref.shape[0]
    x = x_ref[0]
    mod = mod_ref[0]
    sh2, sc2, gt2 = mod[3:4], mod[4:5], mod[5:6]
    h = _rms(x, gpre_ref[...]) * (1.0 + sc2) + sh2
    gu = jnp.dot(h, w1_ref[...], preferred_element_type=f32)
    a = _silu(gu[:, 0:d_ff]) * gu[:, d_ff:2 * d_ff]
    y = jnp.dot(a, w2_ref[...], preferred_element_type=f32)
    o_ref[0] = x + gt2 * _rms(y, gpost_ref[...])


def _const_spec(shape):
    return pl.BlockSpec(shape, lambda *_: (0,) * len(shape))


def _modulation(c, w_ada, b_ada):
    depth, d, n = w_ada.shape
    b = c.shape[0]
    tn = 1024
    return pl.pallas_call(
        _mod_kernel,
        grid=(depth, n // tn),
        in_specs=[pl.BlockSpec((b, d), lambda l, j: (0, 0)),
                  pl.BlockSpec((1, d, tn), lambda l, j: (l, 0, j)),
                  pl.BlockSpec((1, 1, tn), lambda l, j: (l, 0, j))],
        out_specs=pl.BlockSpec((1, b, tn), lambda l, j: (l, 0, j)),
        out_shape=jax.ShapeDtypeStruct((depth, b, n), jnp.float32),
        compiler_params=pltpu.CompilerParams(dimension_semantics=("arbitrary", "arbitrary"),
                                             vmem_limit_bytes=VMEM_LIMIT_BYTES),
        name="adaln_mod",
    )(c, w_ada, b_ada.reshape(depth, 1, n))


def _mixer(x, mod, g_pre, g_post, w_in, conv_w, conv_b, ln_g, ln_b, lower_bounds, norm_g, w_out, zmat,
           *, layer, tile):
    b, s, d = x.shape
    in_width = w_in.shape[2]
    layer_slab = lambda w: pl.BlockSpec((None,) + w.shape[1:], lambda g: (layer, 0, 0),
                                        pipeline_mode=pl.Buffered(1))
    conv_width = conv_w.shape[1]
    depth, hgrn_width = lower_bounds.shape
    heads = hgrn_width // HEAD_DIM
    mix_width = w_out.shape[1]
    tiles_per_row = s // tile
    n_tiles = b * tiles_per_row
    nxt = lambda g: jnp.minimum(g, n_tiles - 1)
    cur = lambda g: jnp.maximum(g - 1, 0)
    kern = functools.partial(_mixer_kernel, layer=layer, depth=depth, tiles_per_row=tiles_per_row)
    out = pl.pallas_call(
        kern,
        grid=(n_tiles + 1,),
        in_specs=[pl.BlockSpec((1, tile, d), lambda g: (nxt(g), 0, 0)),
                  pl.BlockSpec((1, N_MOD, d), lambda g: (nxt(g) // tiles_per_row, 0, 0)),
                  pl.BlockSpec((1, N_MOD, d), lambda g: (cur(g) // tiles_per_row, 0, 0)),
                  _const_spec((1, d)), _const_spec((1, d)),
                  layer_slab(w_in),
                  _const_spec((CONV_KERNEL, conv_width)), _const_spec((1, conv_width)),
                  _const_spec((1, conv_width)), _const_spec((1, conv_width)),
                  _const_spec((depth, hgrn_width)), _const_spec((1, HEAD_DIM)),
                  layer_slab(w_out),
                  _const_spec(zmat.shape)],
        out_specs=pl.BlockSpec((1, tile, d), lambda g: (cur(g), 0, 0)),
        out_shape=jax.ShapeDtypeStruct((n_tiles, tile, d), jnp.float32),
        scratch_shapes=[pltpu.VMEM((tile, in_width), jnp.float32),
                        pltpu.VMEM((tile, in_width), jnp.float32),
                        pltpu.VMEM((tile, d), jnp.float32),
                        pltpu.VMEM((tile, d), jnp.float32),
                        pltpu.VMEM((tile + CONV_HALO, conv_width), jnp.float32),
                        pltpu.VMEM((SUBLANES - 1, tile + CONV_HALO, conv_width), jnp.float32),
                        pltpu.VMEM((CONV_KERNEL, SUBLANES, conv_width), jnp.float32),
                        pltpu.VMEM((heads, HEAD_DIM, HEAD_DIM), jnp.float32),
                        pltpu.VMEM((tile, mix_width), jnp.bfloat16),
                        pltpu.VMEM((tile, d), jnp.bfloat16),
                        pltpu.VMEM((tile, hgrn_width), jnp.float32),
                        pltpu.VMEM((tile, hgrn_width), jnp.float32),
                        pltpu.VMEM((tile // CHUNK, _ALL_ROWS, hgrn_width), jnp.float32),
                        pltpu.VMEM((tile // CHUNK, 2 * CHUNK, hgrn_width), jnp.bfloat16)],
        compiler_params=pltpu.CompilerParams(dimension_semantics=("arbitrary",),
                                             vmem_limit_bytes=VMEM_LIMIT_BYTES),
        name=f"mixer_l{layer}",
    )(x.reshape(n_tiles, tile, d), mod, mod, g_pre, g_post, w_in, conv_w, conv_b, ln_g, ln_b, lower_bounds,
      norm_g, w_out, zmat)
    return out.reshape(b, s, d)


def _ffn(x, mod, g_pre, g_post, w1_all, w2_all, *, layer, tile):
    b, s, d = x.shape
    layer_slab = lambda w: pl.BlockSpec((None,) + w.shape[1:], lambda i, t: (layer, 0, 0),
                                        pipeline_mode=pl.Buffered(1))
    return pl.pallas_call(
        _ffn_kernel,
        grid=(b, s // tile),
        in_specs=[pl.BlockSpec((1, tile, d), lambda i, t: (i, t, 0)),
                  pl.BlockSpec((1, N_MOD, d), lambda i, t: (i, 0, 0)),
                  _const_spec((1, d)), _const_spec((1, d)),
                  layer_slab(w1_all), layer_slab(w2_all)],
        out_specs=pl.BlockSpec((1, tile, d), lambda i, t: (i, t, 0)),
        out_shape=jax.ShapeDtypeStruct((b, s, d), jnp.float32),
        compiler_params=pltpu.CompilerParams(dimension_semantics=("arbitrary", "arbitrary"),
                                             vmem_limit_bytes=VMEM_LIMIT_BYTES),
        name=f"ffn_l{layer}",
    )(x, mod, g_pre, g_post, w1_all, w2_all)


def kernel(x, c, w_ada, b_ada, g_pre_mix, g_post_mix, w_in, conv_w, conv_b, conv_ln_g, conv_ln_b,
           hgrn_lower_bounds, hgrn_norm_g, w_out, g_pre_ffn, g_post_ffn, w_ffn_in, w_ffn_out):
    depth = w_ada.shape[0]
    b, s, d = x.shape
    bf16 = jnp.bfloat16
    mod = _modulation(c, w_ada, b_ada).reshape(depth, b, N_MOD, d)
    zmat = jnp.asarray(_decay_matrix(), dtype=bf16)
    tile = min(256, s)
    w_in_bf, w_out_bf = w_in.astype(bf16), w_out.astype(bf16)
    for l in range(depth):
        row = lambda p: p[l][None, :]
        x = _mixer(x, mod[l], row(g_pre_mix), row(g_post_mix), w_in_bf, conv_w[l],
                   row(conv_b), row(conv_ln_g), row(conv_ln_b), hgrn_lower_bounds, row(hgrn_norm_g),
                   w_out_bf, zmat, layer=l, tile=tile)
        x = _ffn(x, mod[l], row(g_pre_ffn), row(g_post_ffn), w_ffn_in, w_ffn_out, layer=l, tile=tile)
    return x
```

```python
import functools

import numpy as np
import jax
import jax.numpy as jnp
from jax import lax
from jax.experimental import pallas as pl
from jax.experimental.pallas import tpu as pltpu

EPS = 1e-6
CONV_KERNEL = 31
CONV_HALO = 32
CONV_ROWS = 32
SUBLANES = 8
HEAD_DIM = 128
CHUNK = 64
MXU_COLS = 256
LEVELS = (32, 16, 8, 4, 2, 1)
SPLIT_BLOCK = 32
SPLIT_LEVELS = tuple(n for n in LEVELS if n >= SPLIT_BLOCK)
MAX_SPLIT_EXPONENT = 70.0
N_MOD = 6
VMEM_LIMIT_BYTES = 56 * 1024 * 1024

_NT = (((1,), (1,)), ((), ()))
_TN = (((0,), (0,)), ((), ()))


def _level_rows(n):
    t = np.arange(CHUNK)[:, None]
    j = np.arange(CHUNK)[None, :]
    p = (t // (2 * n)) * (2 * n) + n - 1
    upper = (t % (2 * n)) >= n
    return np.where(upper, (j > p) & (j <= t), (j > t) & (j <= p)).astype(np.float32)


def _decay_matrix():
    t = np.arange(CHUNK)[:, None]
    j = np.arange(CHUNK)[None, :]
    mid = (t // SPLIT_BLOCK) * SPLIT_BLOCK + SPLIT_BLOCK // 2 - 1
    rel = ((j > mid) & (j <= t)).astype(np.float32) - ((j > t) & (j <= mid)).astype(np.float32)
    blocks = [(j <= t).astype(np.float32), (j > t).astype(np.float32)]
    blocks += [_level_rows(n) for n in SPLIT_LEVELS] + [rel]
    blocks += [_level_rows(n) for n in LEVELS if n not in SPLIT_LEVELS]
    z = np.concatenate(blocks, axis=0)
    return np.concatenate([z, z], axis=1)


def _level_block(n):
    if n in SPLIT_LEVELS:
        return 2 + SPLIT_LEVELS.index(n)
    rest = [m for m in LEVELS if m not in SPLIT_LEVELS]
    return 3 + len(SPLIT_LEVELS) + rest.index(n)


_REL_BLOCK = 2 + len(SPLIT_LEVELS)
_SHORT_ROWS = (_REL_BLOCK + 1) * CHUNK
_ALL_ROWS = (3 + len(LEVELS)) * CHUNK


def _sigmoid(x):
    return 1.0 / (1.0 + jnp.exp(-x))


def _silu(x):
    return x * _sigmoid(x)


def _rms(x, g):
    return x * lax.rsqrt(jnp.mean(x * x, axis=-1, keepdims=True) + EPS) * g


def _mod_kernel(c_ref, w_ref, b_ref, o_ref):
    c = c_ref[...]
    o_ref[0] = jnp.dot(_silu(c), w_ref[0], preferred_element_type=jnp.float32) + b_ref[0]


def _lockstep(streams):
    active = list(streams)
    while active:
        for s in list(active):
            if next(s, active) is active:
                active.remove(s)


def _projection_input(x_ref, mod_ref, gpre_ref, x_w, h_s):
    x = x_ref[0]
    mod = mod_ref[0]
    sh1, sc1 = mod[0:1], mod[1:2]
    h_s[...] = (_rms(x, gpre_ref[...]) * (1.0 + sc1) + sh1).astype(jnp.bfloat16)
    x_w[...] = x


def _projection_pieces(win_ref, zin_w, h_s):
    for j in range(win_ref.shape[1] // MXU_COLS):
        cs = slice(j * MXU_COLS, (j + 1) * MXU_COLS)
        zin_w[:, cs] = jnp.dot(h_s[...], win_ref[:, cs], preferred_element_type=jnp.float32)
        yield


def _mixer_stage2(zin_s, x_s, mod_ref, gpost_ref, cb_ref, lng_ref, lnb_ref, lb_ref, ng_ref, wout_ref, z_ref,
                  o_ref, ubuf_s, ush_s, cwb_s, st_s, mix_s, q_s, k_s, ex_s, hl_s, flag_s, zin_next, make_filler,
                  *, layer, depth, short_path):
    tile = x_s.shape[0]
    cw = CONV_HALO
    conv_width = cb_ref.shape[1]
    hgrn_width = lb_ref.shape[1]
    heads = hgrn_width // HEAD_DIM
    n_chunks = tile // CHUNK
    n_slots = tile // CONV_ROWS
    heads_per_slot = (n_chunks * heads) // n_slots
    o0 = 2 * conv_width
    f32, bf16 = jnp.float32, jnp.bfloat16

    filler = make_filler()
    u = zin_s[:, 0:conv_width] * _sigmoid(zin_s[:, conv_width:2 * conv_width])
    ubuf_s[cw:cw + tile, :] = u
    next(filler, None)
    for r in range(1, SUBLANES):
        ush_s[r - 1, SUBLANES:cw + tile, :] = ubuf_s[SUBLANES - r:cw + tile - r, :]
        next(filler, None)

    for _ in filler:
        pass

    def prepare_next_tile():
        rows = [lb_ref[i:i + 1, :] for i in range(depth)]
        m = functools.reduce(jnp.maximum, rows)
        e = [jnp.exp(r_ - m) for r_ in rows]
        tot = functools.reduce(jnp.add, e)
        lb = functools.reduce(jnp.add, e[1:layer + 1], jnp.zeros_like(tot)) / tot
        log_lb = jnp.log(lb)
        log_1m = jnp.log(1.0 - lb)
        span = jnp.zeros((1, hgrn_width), f32)
        for c in range(n_chunks):
            r0 = c * CHUNK
            q_s[r0:r0 + CHUNK, :] = _silu(zin_next[r0:r0 + CHUNK, o0:o0 + hgrn_width])
            z = zin_next[r0:r0 + CHUNK, o0 + hgrn_width:o0 + 2 * hgrn_width]
            log_sig = jnp.minimum(z, 0.0) - jnp.log(1.0 + jnp.exp(-jnp.abs(z)))
            t2 = log_1m + log_sig
            log_f = jnp.maximum(log_lb, t2) + jnp.log(1.0 + jnp.exp(-jnp.abs(log_lb - t2)))
            k_s[r0:r0 + CHUNK, :] = (1.0 - lb) * _sigmoid(-z)
            hi = log_f.astype(bf16)
            lo = (log_f - hi.astype(f32)).astype(bf16)
            hl = jnp.concatenate([hi, lo], axis=0)
            hl_s[c] = hl
            ex = jnp.dot(z_ref[0:_SHORT_ROWS, :], hl, preferred_element_type=f32)
            ex_s[c, 0:_SHORT_ROWS, :] = ex
            rel = jnp.abs(ex[_REL_BLOCK * CHUNK:(_REL_BLOCK + 1) * CHUNK, :])
            span = jnp.maximum(span, jnp.max(rel, axis=0, keepdims=True))
        flag_s[0] = (jnp.max(span) <= MAX_SPLIT_EXPONENT).astype(jnp.int32)

    def conv_block(r):
        rb = CONV_ROWS
        acc = jnp.broadcast_to(cb_ref[...], (rb, conv_width)).reshape(rb // SUBLANES, SUBLANES, conv_width)
        for j in range(CONV_KERNEL):
            delay = CONV_KERNEL - 1 - j
            start = r * rb + cw - (delay // SUBLANES) * SUBLANES
            res = delay % SUBLANES
            src = ubuf_s[start:start + rb, :] if res == 0 else ush_s[res - 1, start:start + rb, :]
            acc = acc + cwb_s[j][None] * src.reshape(rb // SUBLANES, SUBLANES, conv_width)
        acc = acc.reshape(rb, conv_width)
        mu = jnp.mean(acc, axis=-1, keepdims=True)
        yield
        d = acc - mu
        var = jnp.mean(d * d, axis=-1, keepdims=True)
        yield
        y = d * lax.rsqrt(var + EPS) * lng_ref[...] + lnb_ref[...]
        mix_s[r * rb:(r + 1) * rb, 0:conv_width] = _silu(y).astype(bf16)

    def level_weight(c, n, ls):
        blk = _level_block(n)
        return jnp.exp(ex_s[c, blk * CHUNK:(blk + 1) * CHUNK, ls]).astype(bf16)

    def level_mask(n, tt, ss):
        return jnp.logical_and(((tt ^ ss) >> (n.bit_length() - 1)) == 1, tt > ss)

    def head(c, hd):
        r0 = c * CHUNK
        rs = slice(r0, r0 + CHUNK)
        ls = slice(hd * HEAD_DIM, (hd + 1) * HEAD_DIM)
        tt = lax.broadcasted_iota(jnp.int32, (CHUNK, CHUNK), 0)
        ss = lax.broadcasted_iota(jnp.int32, (CHUNK, CHUNK), 1)
        qh, kh = q_s[rs, ls], k_s[rs, ls]
        vcol = o0 + 2 * hgrn_width + hd * HEAD_DIM
        qb, kb, vb = qh.astype(bf16), kh.astype(bf16), zin_s[rs, vcol:vcol + HEAD_DIM].astype(bf16)
        b_cum = ex_s[c, 0:CHUNK, ls]
        levels = SPLIT_LEVELS if short_path else LEVELS
        if short_path:
            rel = ex_s[c, _REL_BLOCK * CHUNK:(_REL_BLOCK + 1) * CHUNK, ls]
            prods = [lax.dot_general((qh * jnp.exp(rel)).astype(bf16), (kh * jnp.exp(-rel)).astype(bf16), _NT,
                                     preferred_element_type=f32)]
        else:
            prods = [lax.dot_general(qb, kb, _NT, preferred_element_type=f32)]
        for n in levels:
            w = level_weight(c, n, ls)
            prods.append(lax.dot_general(qb * w, kb * w, _NT, preferred_element_type=f32))
        q_in = (qh * jnp.exp(b_cum)).astype(bf16)
        k_out = (kh * jnp.exp(ex_s[c, CHUNK:2 * CHUNK, ls])).astype(bf16)
        d_st = lax.dot_general(vb, k_out, _TN, preferred_element_type=f32)
        yield
        if short_path:
            inside = jnp.logical_and(tt // SPLIT_BLOCK == ss // SPLIT_BLOCK, tt >= ss)
        else:
            inside = tt == ss
        a = jnp.where(inside, prods[0], 0.0)
        for n, p in zip(levels, prods[1:]):
            a = jnp.where(level_mask(n, tt, ss), p, a)
        st = st_s[hd]
        o = (jnp.dot(a.astype(bf16), vb, preferred_element_type=f32)
             + lax.dot_general(q_in, st.astype(bf16), _NT, preferred_element_type=f32))
        st_s[hd] = st * jnp.exp(b_cum[CHUNK - 1:CHUNK, :]) + d_st
        yield
        ms = jnp.mean(o * o, axis=-1, keepdims=True)
        gcol = o0 + 3 * hgrn_width + hd * HEAD_DIM
        gate = _silu(zin_s[rs, gcol:gcol + HEAD_DIM])
        yield
        o = o * lax.rsqrt(ms + EPS) * ng_ref[...] * gate
        mix_s[rs, conv_width + hd * HEAD_DIM:conv_width + (hd + 1) * HEAD_DIM] = o.astype(bf16)

    def slots():
        if not short_path:
            for c in range(n_chunks):
                ex_s[c, _SHORT_ROWS:, :] = jnp.dot(z_ref[_SHORT_ROWS:, :], hl_s[c], preferred_element_type=f32)
        for s in range(n_slots):
            first = s * heads_per_slot
            streams = [head(i // heads, i % heads) for i in range(first, first + heads_per_slot)]
            streams.append(conv_block(s))
            _lockstep(streams)

    slots()
    ubuf_s[0:cw, :] = ubuf_s[tile:tile + cw, :]
    mixed = jnp.dot(mix_s[...], wout_ref[...], preferred_element_type=f32)
    gt1 = mod_ref[0][2:3]
    o_ref[0] = x_s[...] + gt1 * _rms(mixed, gpost_ref[...])
    prepare_next_tile()


def _mixer_kernel(xn_ref, modn_ref, modc_ref, gpre_ref, gpost_ref, win_ref, cw_ref, cb_ref, lng_ref, lnb_ref,
                  lb_ref, ng_ref, wout_ref, z_ref, o_ref, zin_a, zin_b, x_a, x_b, ubuf_s, ush_s, cwb_s, st_s,
                  mix_s, h_s, q_s, k_s, ex_s, hl_s, flag_s, *, layer, depth, tiles_per_row):
    g = pl.program_id(0)
    f32 = jnp.float32

    @pl.when(g == 0)
    def _():
        zin_b[...] = jnp.zeros(zin_b.shape, f32)
        x_b[...] = jnp.zeros(x_b.shape, f32)
        q_s[...] = jnp.zeros(q_s.shape, f32)
        k_s[...] = jnp.zeros(k_s.shape, f32)
        ex_s[...] = jnp.zeros(ex_s.shape, f32)
        hl_s[...] = jnp.zeros(hl_s.shape, hl_s.dtype)
        flag_s[0] = jnp.int32(1)
        for j in range(CONV_KERNEL):
            cwb_s[j] = jnp.broadcast_to(cw_ref[j:j + 1, :], (SUBLANES, cw_ref.shape[1]))

    @pl.when(jnp.maximum(g - 1, 0) % tiles_per_row == 0)
    def _():
        ubuf_s[0:CONV_HALO, :] = jnp.zeros((CONV_HALO, ubuf_s.shape[1]), f32)
        st_s[...] = jnp.zeros(st_s.shape, f32)

    def step(zin_w, x_w, zin_r, x_r, short_path):
        _projection_input(xn_ref, modn_ref, gpre_ref, x_w, h_s)
        _mixer_stage2(zin_r, x_r, modc_ref, gpost_ref, cb_ref, lng_ref, lnb_ref, lb_ref, ng_ref, wout_ref,
                      z_ref, o_ref, ubuf_s, ush_s, cwb_s, st_s, mix_s, q_s, k_s, ex_s, hl_s, flag_s, zin_w,
                      functools.partial(_projection_pieces, win_ref, zin_w, h_s),
                      layer=layer, depth=depth, short_path=short_path)

    short = flag_s[0] == 1
    even = g % 2 == 0
    for parity_even, bufs in ((True, (zin_a, x_a, zin_b, x_b)), (False, (zin_b, x_b, zin_a, x_a))):
        for short_path in (True, False):
            cond = jnp.logical_and(even == parity_even, short == short_path)
            pl.when(cond)(functools.partial(step, *bufs, short_path))


def _ffn_kernel(x_ref, mod_ref, gpre_ref, gpost_ref, w1_ref, w2_ref, o_ref):
    f32 = jnp.float32
    d_ff = w2_ref.shape[0]
    x = x_ref[0]
    mod = mod_ref[0]
    sh2, sc2, gt2 = mod[3:4], mod[4:5], mod[5:6]
    h = _rms(x, gpre_ref[...]) * (1.0 + sc2) + sh2
    gu = jnp.dot(h, w1_ref[...], preferred_element_type=f32)
    a = _silu(gu[:, 0:d_ff]) * gu[:, d_ff:2 * d_ff]
    y = jnp.dot(a, w2_ref[...], preferred_element_type=f32)
    o_ref[0] = x + gt2 * _rms(y, gpost_ref[...])


def _const_spec(shape):
    return pl.BlockSpec(shape, lambda *_: (0,) * len(shape))


def _modulation(c, w_ada, b_ada):
    depth, d, n = w_ada.shape
    b = c.shape[0]
    tn = 1024
    return pl.pallas_call(
        _mod_kernel,
        grid=(depth, n // tn),
        in_specs=[pl.BlockSpec((b, d), lambda l, j: (0, 0)),
                  pl.BlockSpec((1, d, tn), lambda l, j: (l, 0, j)),
                  pl.BlockSpec((1, 1, tn), lambda l, j: (l, 0, j))],
        out_specs=pl.BlockSpec((1, b, tn), lambda l, j: (l, 0, j)),
        out_shape=jax.ShapeDtypeStruct((depth, b, n), jnp.float32),
        compiler_params=pltpu.CompilerParams(dimension_semantics=("arbitrary", "arbitrary"),
                                             vmem_limit_bytes=VMEM_LIMIT_BYTES),
        name="adaln_mod",
    )(c, w_ada, b_ada.reshape(depth, 1, n))


def _mixer(x, mod, g_pre, g_post, w_in, conv_w, conv_b, ln_g, ln_b, lower_bounds, norm_g, w_out, zmat,
           *, layer, tile):
    b, s, d = x.shape
    in_width = w_in.shape[2]
    layer_slab = lambda w: pl.BlockSpec((None,) + w.shape[1:], lambda g: (layer, 0, 0),
                                        pipeline_mode=pl.Buffered(1))
    conv_width = conv_w.shape[1]
    depth, hgrn_width = lower_bounds.shape
    heads = hgrn_width // HEAD_DIM
    mix_width = w_out.shape[1]
    tiles_per_row = s // tile
    n_tiles = b * tiles_per_row
    nxt = lambda g: jnp.minimum(g, n_tiles - 1)
    cur = lambda g: jnp.maximum(g - 1, 0)
    kern = functools.partial(_mixer_kernel, layer=layer, depth=depth, tiles_per_row=tiles_per_row)
    out = pl.pallas_call(
        kern,
        grid=(n_tiles + 1,),
        in_specs=[pl.BlockSpec((1, tile, d), lambda g: (nxt(g), 0, 0)),
                  pl.BlockSpec((1, N_MOD, d), lambda g: (nxt(g) // tiles_per_row, 0, 0)),
                  pl.BlockSpec((1, N_MOD, d), lambda g: (cur(g) // tiles_per_row, 0, 0)),
                  _const_spec((1, d)), _const_spec((1, d)),
                  layer_slab(w_in),
                  _const_spec((CONV_KERNEL, conv_width)), _const_spec((1, conv_width)),
                  _const_spec((1, conv_width)), _const_spec((1, conv_width)),
                  _const_spec((depth, hgrn_width)), _const_spec((1, HEAD_DIM)),
                  layer_slab(w_out),
                  _const_spec(zmat.shape)],
        out_specs=pl.BlockSpec((1, tile, d), lambda g: (cur(g), 0, 0)),
        out_shape=jax.ShapeDtypeStruct((n_tiles, tile, d), jnp.float32),
        scratch_shapes=[pltpu.VMEM((tile, in_width), jnp.float32),
                        pltpu.VMEM((tile, in_width), jnp.float32),
                        pltpu.VMEM((tile, d), jnp.float32),
                        pltpu.VMEM((tile, d), jnp.float32),
                        pltpu.VMEM((tile + CONV_HALO, conv_width), jnp.float32),
                        pltpu.VMEM((SUBLANES - 1, tile + CONV_HALO, conv_width), jnp.float32),
                        pltpu.VMEM((CONV_KERNEL, SUBLANES, conv_width), jnp.float32),
                        pltpu.VMEM((heads, HEAD_DIM, HEAD_DIM), jnp.float32),
                        pltpu.VMEM((tile, mix_width), jnp.bfloat16),
                        pltpu.VMEM((tile, d), jnp.bfloat16),
                        pltpu.VMEM((tile, hgrn_width), jnp.float32),
                        pltpu.VMEM((tile, hgrn_width), jnp.float32),
                        pltpu.VMEM((tile // CHUNK, _ALL_ROWS, hgrn_width), jnp.float32),
                        pltpu.VMEM((tile // CHUNK, 2 * CHUNK, hgrn_width), jnp.bfloat16),
                        pltpu.SMEM((1,), jnp.int32)],
        compiler_params=pltpu.CompilerParams(dimension_semantics=("arbitrary",),
                                             vmem_limit_bytes=VMEM_LIMIT_BYTES),
        name=f"mixer_l{layer}",
    )(x.reshape(n_tiles, tile, d), mod, mod, g_pre, g_post, w_in, conv_w, conv_b, ln_g, ln_b, lower_bounds,
      norm_g, w_out, zmat)
    return out.reshape(b, s, d)


def _ffn(x, mod, g_pre, g_post, w1_all, w2_all, *, layer, tile):
    b, s, d = x.shape
    layer_slab = lambda w: pl.BlockSpec((None,) + w.shape[1:], lambda i, t: (layer, 0, 0),
                                        pipeline_mode=pl.Buffered(1))
    return pl.pallas_call(
        _ffn_kernel,
        grid=(b, s // tile),
        in_specs=[pl.BlockSpec((1, tile, d), lambda i, t: (i, t, 0)),
                  pl.BlockSpec((1, N_MOD, d), lambda i, t: (i, 0, 0)),
                  _const_spec((1, d)), _const_spec((1, d)),
                  layer_slab(w1_all), layer_slab(w2_all)],
        out_specs=pl.BlockSpec((1, tile, d), lambda i, t: (i, t, 0)),
        out_shape=jax.ShapeDtypeStruct((b, s, d), jnp.float32),
        compiler_params=pltpu.CompilerParams(dimension_semantics=("arbitrary", "arbitrary"),
                                             vmem_limit_bytes=VMEM_LIMIT_BYTES),
        name=f"ffn_l{layer}",
    )(x, mod, g_pre, g_post, w1_all, w2_all)


def kernel(x, c, w_ada, b_ada, g_pre_mix, g_post_mix, w_in, conv_w, conv_b, conv_ln_g, conv_ln_b,
           hgrn_lower_bounds, hgrn_norm_g, w_out, g_pre_ffn, g_post_ffn, w_ffn_in, w_ffn_out):
    depth = w_ada.shape[0]
    b, s, d = x.shape
    bf16 = jnp.bfloat16
    mod = _modulation(c, w_ada, b_ada).reshape(depth, b, N_MOD, d)
    zmat = jnp.asarray(_decay_matrix(), dtype=bf16)
    tile = min(256, s)
    w_in_bf, w_out_bf = w_in.astype(bf16), w_out.astype(bf16)
    for l in range(depth):
        row = lambda p: p[l][None, :]
        x = _mixer(x, mod[l], row(g_pre_mix), row(g_post_mix), w_in_bf, conv_w[l],
                   row(conv_b), row(conv_ln_g), row(conv_ln_b), hgrn_lower_bounds, row(hgrn_norm_g),
                   w_out_bf, zmat, layer=l, tile=tile)
        x = _ffn(x, mod[l], row(g_pre_ffn), row(g_post_ffn), w_ffn_in, w_ffn_out, layer=l, tile=tile)
    return x
```

```python
import functools

import numpy as np
import jax
import jax.numpy as jnp
from jax import lax
from jax.experimental import pallas as pl
from jax.experimental.pallas import tpu as pltpu

EPS = 1e-6
CONV_KERNEL = 31
CONV_HALO = 32
CONV_ROWS = 16
SUBLANES = 8
HEAD_DIM = 128
CHUNK = 64
MXU_COLS = 256
LEVELS = (32, 16, 8, 4, 2, 1)
SPLIT_BLOCK = 32
SPLIT_LEVELS = tuple(n for n in LEVELS if n >= SPLIT_BLOCK)
MAX_SPLIT_EXPONENT = 70.0
N_MOD = 6
VMEM_LIMIT_BYTES = 56 * 1024 * 1024

_NT = (((1,), (1,)), ((), ()))
_TN = (((0,), (0,)), ((), ()))


def _level_rows(n):
    t = np.arange(CHUNK)[:, None]
    j = np.arange(CHUNK)[None, :]
    p = (t // (2 * n)) * (2 * n) + n - 1
    upper = (t % (2 * n)) >= n
    return np.where(upper, (j > p) & (j <= t), (j > t) & (j <= p)).astype(np.float32)


def _decay_matrix():
    t = np.arange(CHUNK)[:, None]
    j = np.arange(CHUNK)[None, :]
    mid = (t // SPLIT_BLOCK) * SPLIT_BLOCK + SPLIT_BLOCK // 2 - 1
    rel = ((j > mid) & (j <= t)).astype(np.float32) - ((j > t) & (j <= mid)).astype(np.float32)
    blocks = [(j <= t).astype(np.float32), (j > t).astype(np.float32)]
    blocks += [_level_rows(n) for n in SPLIT_LEVELS] + [rel]
    blocks += [_level_rows(n) for n in LEVELS if n not in SPLIT_LEVELS]
    z = np.concatenate(blocks, axis=0)
    return np.concatenate([z, z], axis=1)


def _level_block(n):
    if n in SPLIT_LEVELS:
        return 2 + SPLIT_LEVELS.index(n)
    rest = [m for m in LEVELS if m not in SPLIT_LEVELS]
    return 3 + len(SPLIT_LEVELS) + rest.index(n)


_REL_BLOCK = 2 + len(SPLIT_LEVELS)
_SHORT_ROWS = (_REL_BLOCK + 1) * CHUNK
_ALL_ROWS = (3 + len(LEVELS)) * CHUNK


def _sigmoid(x):
    return 1.0 / (1.0 + jnp.exp(-x))


def _silu(x):
    return x * _sigmoid(x)


def _rms(x, g):
    return x * lax.rsqrt(jnp.mean(x * x, axis=-1, keepdims=True) + EPS) * g


def _mod_kernel(c_ref, w_ref, b_ref, o_ref):
    c = c_ref[...]
    o_ref[0] = jnp.dot(_silu(c), w_ref[0], preferred_element_type=jnp.float32) + b_ref[0]


def _lockstep(streams):
    active = list(streams)
    while active:
        for s in list(active):
            if next(s, active) is active:
                active.remove(s)


def _projection_input(x_ref, mod_ref, gpre_ref, x_w, h_s):
    x = x_ref[0]
    mod = mod_ref[0]
    sh1, sc1 = mod[0:1], mod[1:2]
    h_s[...] = (_rms(x, gpre_ref[...] * (1.0 + sc1)) + sh1).astype(jnp.bfloat16)
    x_w[...] = x


def _projection_pieces(win_ref, zin_w, h_s):
    for j in range(win_ref.shape[1] // MXU_COLS):
        cs = slice(j * MXU_COLS, (j + 1) * MXU_COLS)
        zin_w[:, cs] = jnp.dot(h_s[...], win_ref[:, cs], preferred_element_type=jnp.float32)
        yield


def _mixer_stage2(zin_s, x_s, mod_ref, gpost_ref, cb_ref, lng_ref, lnb_ref, lb_ref, ng_ref, wout_ref, z_ref,
                  o_ref, ubuf_s, ush_s, cwb_s, st_s, mix_s, q_s, k_s, ex_s, hl_s, flag_s, zin_next, make_filler,
                  *, layer, depth, short_path):
    tile = x_s.shape[0]
    cw = CONV_HALO
    conv_width = cb_ref.shape[1]
    hgrn_width = lb_ref.shape[1]
    heads = hgrn_width // HEAD_DIM
    n_chunks = tile // CHUNK
    n_slots = n_chunks * heads
    n_conv = tile // CONV_ROWS
    o0 = 2 * conv_width
    f32, bf16 = jnp.float32, jnp.bfloat16

    filler = make_filler()
    u = zin_s[:, 0:conv_width] * _sigmoid(zin_s[:, conv_width:2 * conv_width])
    ubuf_s[cw:cw + tile, :] = u
    next(filler, None)
    for r in range(1, SUBLANES):
        ush_s[r - 1, SUBLANES:cw + tile, :] = ubuf_s[SUBLANES - r:cw + tile - r, :]
        next(filler, None)

    for _ in filler:
        pass

    def prepare_next_tile():
        rows = [lb_ref[i:i + 1, :] for i in range(depth)]
        m = functools.reduce(jnp.maximum, rows)
        e = [jnp.exp(r_ - m) for r_ in rows]
        tot = functools.reduce(jnp.add, e)
        lb = functools.reduce(jnp.add, e[1:layer + 1], jnp.zeros_like(tot)) / tot
        log_lb = jnp.log(lb)
        log_1m = jnp.log(1.0 - lb)
        span = jnp.zeros((1, hgrn_width), f32)
        for c in range(n_chunks):
            r0 = c * CHUNK
            q_s[r0:r0 + CHUNK, :] = _silu(zin_next[r0:r0 + CHUNK, o0:o0 + hgrn_width])
            z = zin_next[r0:r0 + CHUNK, o0 + hgrn_width:o0 + 2 * hgrn_width]
            log_sig = jnp.minimum(z, 0.0) - jnp.log(1.0 + jnp.exp(-jnp.abs(z)))
            t2 = log_1m + log_sig
            log_f = jnp.maximum(log_lb, t2) + jnp.log(1.0 + jnp.exp(-jnp.abs(log_lb - t2)))
            k_s[r0:r0 + CHUNK, :] = (1.0 - lb) * _sigmoid(-z)
            hi = log_f.astype(bf16)
            lo = (log_f - hi.astype(f32)).astype(bf16)
            hl = jnp.concatenate([hi, lo], axis=0)
            hl_s[c] = hl
            ex = jnp.dot(z_ref[0:_SHORT_ROWS, :], hl, preferred_element_type=f32)
            ex_s[c, 0:_SHORT_ROWS, :] = ex
            rel = jnp.abs(ex[_REL_BLOCK * CHUNK:(_REL_BLOCK + 1) * CHUNK, :])
            span = jnp.maximum(span, jnp.max(rel, axis=0, keepdims=True))
        flag_s[0] = (jnp.max(span) <= MAX_SPLIT_EXPONENT).astype(jnp.int32)

    def conv_block(r):
        rb = CONV_ROWS
        acc = jnp.broadcast_to(cb_ref[...], (rb, conv_width)).reshape(rb // SUBLANES, SUBLANES, conv_width)
        for j in range(CONV_KERNEL):
            delay = CONV_KERNEL - 1 - j
            start = r * rb + cw - (delay // SUBLANES) * SUBLANES
            res = delay % SUBLANES
            src = ubuf_s[start:start + rb, :] if res == 0 else ush_s[res - 1, start:start + rb, :]
            acc = acc + cwb_s[j][None] * src.reshape(rb // SUBLANES, SUBLANES, conv_width)
        acc = acc.reshape(rb, conv_width)
        mu = jnp.mean(acc, axis=-1, keepdims=True)
        yield
        d = acc - mu
        var = jnp.mean(d * d, axis=-1, keepdims=True)
        yield
        y = d * lax.rsqrt(var + EPS) * lng_ref[...] + lnb_ref[...]
        mix_s[r * rb:(r + 1) * rb, 0:conv_width] = _silu(y).astype(bf16)

    def level_weight(c, n, ls):
        blk = _level_block(n)
        return jnp.exp(ex_s[c, blk * CHUNK:(blk + 1) * CHUNK, ls]).astype(bf16)

    def level_mask(n, tt, ss):
        return jnp.logical_and(((tt ^ ss) >> (n.bit_length() - 1)) == 1, tt > ss)

    def head(c, hd):
        r0 = c * CHUNK
        rs = slice(r0, r0 + CHUNK)
        ls = slice(hd * HEAD_DIM, (hd + 1) * HEAD_DIM)
        tt = lax.broadcasted_iota(jnp.int32, (CHUNK, CHUNK), 0)
        ss = lax.broadcasted_iota(jnp.int32, (CHUNK, CHUNK), 1)
        qh, kh = q_s[rs, ls], k_s[rs, ls]
        vcol = o0 + 2 * hgrn_width + hd * HEAD_DIM
        qb, kb, vb = qh.astype(bf16), kh.astype(bf16), zin_s[rs, vcol:vcol + HEAD_DIM].astype(bf16)
        b_cum = ex_s[c, 0:CHUNK, ls]
        levels = SPLIT_LEVELS if short_path else LEVELS
        if short_path:
            rel = ex_s[c, _REL_BLOCK * CHUNK:(_REL_BLOCK + 1) * CHUNK, ls]
            prods = [lax.dot_general((qh * jnp.exp(rel)).astype(bf16), (kh * jnp.exp(-rel)).astype(bf16), _NT,
                                     preferred_element_type=f32)]
        else:
            prods = [lax.dot_general(qb, kb, _NT, preferred_element_type=f32)]
        for n in levels:
            w = level_weight(c, n, ls)
            prods.append(lax.dot_general(qb * w, kb * w, _NT, preferred_element_type=f32))
        q_in = (qh * jnp.exp(b_cum)).astype(bf16)
        k_out = (kh * jnp.exp(ex_s[c, CHUNK:2 * CHUNK, ls])).astype(bf16)
        d_st = lax.dot_general(vb, k_out, _TN, preferred_element_type=f32)
        yield
        if short_path:
            inside = jnp.logical_and(tt // SPLIT_BLOCK == ss // SPLIT_BLOCK, tt >= ss)
        else:
            inside = tt == ss
        a = jnp.where(inside, prods[0], 0.0)
        for n, p in zip(levels, prods[1:]):
            a = jnp.where(level_mask(n, tt, ss), p, a)
        st = st_s[hd]
        o = (jnp.dot(a.astype(bf16), vb, preferred_element_type=f32)
             + lax.dot_general(q_in, st.astype(bf16), _NT, preferred_element_type=f32))
        st_s[hd] = st * jnp.exp(b_cum[CHUNK - 1:CHUNK, :]) + d_st
        yield
        ms = jnp.mean(o * o, axis=-1, keepdims=True)
        gcol = o0 + 3 * hgrn_width + hd * HEAD_DIM
        gate = _silu(zin_s[rs, gcol:gcol + HEAD_DIM])
        yield
        o = o * lax.rsqrt(ms + EPS) * ng_ref[...] * gate
        mix_s[rs, conv_width + hd * HEAD_DIM:conv_width + (hd + 1) * HEAD_DIM] = o.astype(bf16)

    def slots():
        if not short_path:
            for c in range(n_chunks):
                ex_s[c, _SHORT_ROWS:, :] = jnp.dot(z_ref[_SHORT_ROWS:, :], hl_s[c], preferred_element_type=f32)
        for s in range(n_slots):
            streams = [head(s // heads, s % heads)]
            streams += [conv_block(r) for r in range(s * n_conv // n_slots, (s + 1) * n_conv // n_slots)]
            _lockstep(streams)

    slots()
    ubuf_s[0:cw, :] = ubuf_s[tile:tile + cw, :]
    mixed = jnp.dot(mix_s[...], wout_ref[...], preferred_element_type=f32)
    gt1 = mod_ref[0][2:3]
    o_ref[0] = x_s[...] + _rms(mixed, gt1 * gpost_ref[...])
    prepare_next_tile()


def _mixer_kernel(xn_ref, modn_ref, modc_ref, gpre_ref, gpost_ref, win_ref, cw_ref, cb_ref, lng_ref, lnb_ref,
                  lb_ref, ng_ref, wout_ref, z_ref, o_ref, zin_a, zin_b, x_a, x_b, ubuf_s, ush_s, cwb_s, st_s,
                  mix_s, h_s, q_s, k_s, ex_s, hl_s, flag_s, *, layer, depth, tiles_per_row):
    g = pl.program_id(0)
    f32 = jnp.float32

    @pl.when(g == 0)
    def _():
        zin_b[...] = jnp.zeros(zin_b.shape, f32)
        x_b[...] = jnp.zeros(x_b.shape, f32)
        q_s[...] = jnp.zeros(q_s.shape, f32)
        k_s[...] = jnp.zeros(k_s.shape, f32)
        ex_s[...] = jnp.zeros(ex_s.shape, f32)
        hl_s[...] = jnp.zeros(hl_s.shape, hl_s.dtype)
        flag_s[0] = jnp.int32(1)
        for j in range(CONV_KERNEL):
            cwb_s[j] = jnp.broadcast_to(cw_ref[j:j + 1, :], (SUBLANES, cw_ref.shape[1]))

    @pl.when(jnp.maximum(g - 1, 0) % tiles_per_row == 0)
    def _():
        ubuf_s[0:CONV_HALO, :] = jnp.zeros((CONV_HALO, ubuf_s.shape[1]), f32)
        st_s[...] = jnp.zeros(st_s.shape, f32)

    def step(zin_w, x_w, zin_r, x_r, short_path):
        _projection_input(xn_ref, modn_ref, gpre_ref, x_w, h_s)
        _mixer_stage2(zin_r, x_r, modc_ref, gpost_ref, cb_ref, lng_ref, lnb_ref, lb_ref, ng_ref, wout_ref,
                      z_ref, o_ref, ubuf_s, ush_s, cwb_s, st_s, mix_s, q_s, k_s, ex_s, hl_s, flag_s, zin_w,
                      functools.partial(_projection_pieces, win_ref, zin_w, h_s),
                      layer=layer, depth=depth, short_path=short_path)

    short = flag_s[0] == 1
    even = g % 2 == 0
    for parity_even, bufs in ((True, (zin_a, x_a, zin_b, x_b)), (False, (zin_b, x_b, zin_a, x_a))):
        for short_path in (True, False):
            cond = jnp.logical_and(even == parity_even, short == short_path)
            pl.when(cond)(functools.partial(step, *bufs, short_path))


def _ffn_kernel(x_ref, mod_ref, gpre_ref, gpost_ref, w1_ref, w2_ref, o_ref):
    f32 = jnp.float32
    d_ff = w2_ref.shape[0]
    x = x_ref[0]
    mod = mod_ref[0]
    sh2, sc2, gt2 = mod[3:4], mod[4:5], mod[5:6]
    h = _rms(x, gpre_ref[...] * (1.0 + sc2)) + sh2
    gu = jnp.dot(h, w1_ref[...], preferred_element_type=f32)
    a = _silu(gu[:, 0:d_ff]) * gu[:, d_ff:2 * d_ff]
    y = jnp.dot(a, w2_ref[...], preferred_element_type=f32)
    o_ref[0] = x + _rms(y, gt2 * gpost_ref[...])


def _const_spec(shape):
    return pl.BlockSpec(shape, lambda *_: (0,) * len(shape))


def _modulation(c, w_ada, b_ada):
    depth, d, n = w_ada.shape
    b = c.shape[0]
    tn = 1024
    return pl.pallas_call(
        _mod_kernel,
        grid=(depth, n // tn),
        in_specs=[pl.BlockSpec((b, d), lambda l, j: (0, 0)),
                  pl.BlockSpec((1, d, tn), lambda l, j: (l, 0, j)),
                  pl.BlockSpec((1, 1, tn), lambda l, j: (l, 0, j))],
        out_specs=pl.BlockSpec((1, b, tn), lambda l, j: (l, 0, j)),
        out_shape=jax.ShapeDtypeStruct((depth, b, n), jnp.float32),
        compiler_params=pltpu.CompilerParams(dimension_semantics=("arbitrary", "arbitrary"),
                                             vmem_limit_bytes=VMEM_LIMIT_BYTES),
        name="adaln_mod",
    )(c, w_ada, b_ada.reshape(depth, 1, n))


def _mixer(x, mod, g_pre, g_post, w_in, conv_w, conv_b, ln_g, ln_b, lower_bounds, norm_g, w_out, zmat,
           *, layer, tile):
    b, s, d = x.shape
    in_width = w_in.shape[2]
    layer_slab = lambda w: pl.BlockSpec((None,) + w.shape[1:], lambda g: (layer, 0, 0),
                                        pipeline_mode=pl.Buffered(1))
    conv_width = conv_w.shape[1]
    depth, hgrn_width = lower_bounds.shape
    heads = hgrn_width // HEAD_DIM
    mix_width = w_out.shape[1]
    tiles_per_row = s // tile
    n_tiles = b * tiles_per_row
    nxt = lambda g: jnp.minimum(g, n_tiles - 1)
    cur = lambda g: jnp.maximum(g - 1, 0)
    kern = functools.partial(_mixer_kernel, layer=layer, depth=depth, tiles_per_row=tiles_per_row)
    out = pl.pallas_call(
        kern,
        grid=(n_tiles + 1,),
        in_specs=[pl.BlockSpec((1, tile, d), lambda g: (nxt(g), 0, 0)),
                  pl.BlockSpec((1, N_MOD, d), lambda g: (nxt(g) // tiles_per_row, 0, 0)),
                  pl.BlockSpec((1, N_MOD, d), lambda g: (cur(g) // tiles_per_row, 0, 0)),
                  _const_spec((1, d)), _const_spec((1, d)),
                  layer_slab(w_in),
                  _const_spec((CONV_KERNEL, conv_width)), _const_spec((1, conv_width)),
                  _const_spec((1, conv_width)), _const_spec((1, conv_width)),
                  _const_spec((depth, hgrn_width)), _const_spec((1, HEAD_DIM)),
                  layer_slab(w_out),
                  _const_spec(zmat.shape)],
        out_specs=pl.BlockSpec((1, tile, d), lambda g: (cur(g), 0, 0)),
        out_shape=jax.ShapeDtypeStruct((n_tiles, tile, d), jnp.float32),
        scratch_shapes=[pltpu.VMEM((tile, in_width), jnp.float32),
                        pltpu.VMEM((tile, in_width), jnp.float32),
                        pltpu.VMEM((tile, d), jnp.float32),
                        pltpu.VMEM((tile, d), jnp.float32),
                        pltpu.VMEM((tile + CONV_HALO, conv_width), jnp.float32),
                        pltpu.VMEM((SUBLANES - 1, tile + CONV_HALO, conv_width), jnp.float32),
                        pltpu.VMEM((CONV_KERNEL, SUBLANES, conv_width), jnp.float32),
                        pltpu.VMEM((heads, HEAD_DIM, HEAD_DIM), jnp.float32),
                        pltpu.VMEM((tile, mix_width), jnp.bfloat16),
                        pltpu.VMEM((tile, d), jnp.bfloat16),
                        pltpu.VMEM((tile, hgrn_width), jnp.float32),
                        pltpu.VMEM((tile, hgrn_width), jnp.float32),
                        pltpu.VMEM((tile // CHUNK, _ALL_ROWS, hgrn_width), jnp.float32),
                        pltpu.VMEM((tile // CHUNK, 2 * CHUNK, hgrn_width), jnp.bfloat16),
                        pltpu.SMEM((1,), jnp.int32)],
        compiler_params=pltpu.CompilerParams(dimension_semantics=("arbitrary",),
                                             vmem_limit_bytes=VMEM_LIMIT_BYTES),
        name=f"mixer_l{layer}",
    )(x.reshape(n_tiles, tile, d), mod, mod, g_pre, g_post, w_in, conv_w, conv_b, ln_g, ln_b, lower_bounds,
      norm_g, w_out, zmat)
    return out.reshape(b, s, d)


def _ffn(x, mod, g_pre, g_post, w1_all, w2_all, *, layer, tile):
    b, s, d = x.shape
    layer_slab = lambda w: pl.BlockSpec((None,) + w.shape[1:], lambda i, t: (layer, 0, 0),
                                        pipeline_mode=pl.Buffered(1))
    return pl.pallas_call(
        _ffn_kernel,
        grid=(b, s // tile),
        in_specs=[pl.BlockSpec((1, tile, d), lambda i, t: (i, t, 0)),
                  pl.BlockSpec((1, N_MOD, d), lambda i, t: (i, 0, 0)),
                  _const_spec((1, d)), _const_spec((1, d)),
                  layer_slab(w1_all), layer_slab(w2_all)],
        out_specs=pl.BlockSpec((1, tile, d), lambda i, t: (i, t, 0)),
        out_shape=jax.ShapeDtypeStruct((b, s, d), jnp.float32),
        compiler_params=pltpu.CompilerParams(dimension_semantics=("arbitrary", "arbitrary"),
                                             vmem_limit_bytes=VMEM_LIMIT_BYTES),
        name=f"ffn_l{layer}",
    )(x, mod, g_pre, g_post, w1_all, w2_all)


def kernel(x, c, w_ada, b_ada, g_pre_mix, g_post_mix, w_in, conv_w, conv_b, conv_ln_g, conv_ln_b,
           hgrn_lower_bounds, hgrn_norm_g, w_out, g_pre_ffn, g_post_ffn, w_ffn_in, w_ffn_out):
    depth = w_ada.shape[0]
    b, s, d = x.shape
    bf16 = jnp.bfloat16
    mod = _modulation(c, w_ada, b_ada).reshape(depth, b, N_MOD, d)
    zmat = jnp.asarray(_decay_matrix(), dtype=bf16)
    tile = min(256, s)
    w_in_bf, w_out_bf = w_in.astype(bf16), w_out.astype(bf16)
    for l in range(depth):
        row = lambda p: p[l][None, :]
        x = _mixer(x, mod[l], row(g_pre_mix), row(g_post_mix), w_in_bf, conv_w[l],
                   row(conv_b), row(conv_ln_g), row(conv_ln_b), hgrn_lower_bounds, row(hgrn_norm_g),
                   w_out_bf, zmat, layer=l, tile=tile)
        x = _ffn(x, mod[l], row(g_pre_ffn), row(g_post_ffn), w_ffn_in, w_ffn_out, layer=l, tile=tile)
    return x
```

```python
import functools

import numpy as np
import jax
import jax.numpy as jnp
from jax import lax
from jax.experimental import pallas as pl
from jax.experimental.pallas import tpu as pltpu

EPS = 1e-6
CONV_KERNEL = 31
CONV_HALO = 32
CONV_ROWS = 16
SUBLANES = 8
HEAD_DIM = 128
CHUNK = 64
MXU_COLS = 256
LEVELS = (32, 16, 8, 4, 2, 1)
SPLIT_BLOCK = 32
SPLIT_LEVELS = tuple(n for n in LEVELS if n >= SPLIT_BLOCK)
MAX_SPLIT_EXPONENT = 70.0
N_MOD = 6
VMEM_LIMIT_BYTES = 56 * 1024 * 1024

_NT = (((1,), (1,)), ((), ()))
_TN = (((0,), (0,)), ((), ()))


def _level_rows(n):
    t = np.arange(CHUNK)[:, None]
    j = np.arange(CHUNK)[None, :]
    p = (t // (2 * n)) * (2 * n) + n - 1
    upper = (t % (2 * n)) >= n
    return np.where(upper, (j > p) & (j <= t), (j > t) & (j <= p)).astype(np.float32)


def _decay_matrix():
    t = np.arange(CHUNK)[:, None]
    j = np.arange(CHUNK)[None, :]
    mid = (t // SPLIT_BLOCK) * SPLIT_BLOCK + SPLIT_BLOCK // 2 - 1
    rel = ((j > mid) & (j <= t)).astype(np.float32) - ((j > t) & (j <= mid)).astype(np.float32)
    blocks = [(j <= t).astype(np.float32), (j > t).astype(np.float32)]
    blocks += [_level_rows(n) for n in SPLIT_LEVELS] + [rel]
    blocks += [_level_rows(n) for n in LEVELS if n not in SPLIT_LEVELS]
    z = np.concatenate(blocks, axis=0)
    return np.concatenate([z, z], axis=1)


def _level_block(n):
    if n in SPLIT_LEVELS:
        return 2 + SPLIT_LEVELS.index(n)
    rest = [m for m in LEVELS if m not in SPLIT_LEVELS]
    return 3 + len(SPLIT_LEVELS) + rest.index(n)


_REL_BLOCK = 2 + len(SPLIT_LEVELS)
_SHORT_ROWS = (_REL_BLOCK + 1) * CHUNK
_ALL_ROWS = (3 + len(LEVELS)) * CHUNK


def _sigmoid(x):
    return 1.0 / (1.0 + jnp.exp(-x))


def _silu(x):
    return x * _sigmoid(x)


def _rms(x, g):
    return x * lax.rsqrt(jnp.mean(x * x, axis=-1, keepdims=True) + EPS) * g


def _mod_kernel(c_ref, w_ref, b_ref, o_ref):
    c = c_ref[...]
    o_ref[0] = jnp.dot(_silu(c), w_ref[0], preferred_element_type=jnp.float32) + b_ref[0]


def _lockstep(streams):
    active = list(streams)
    while active:
        for s in list(active):
            if next(s, active) is active:
                active.remove(s)


def _projection_input(x_ref, mod_ref, gpre_ref, x_w, h_s):
    x = x_ref[0]
    mod = mod_ref[0]
    sh1, sc1 = mod[0:1], mod[1:2]
    h_s[...] = (_rms(x, gpre_ref[...] * (1.0 + sc1)) + sh1).astype(jnp.bfloat16)
    x_w[...] = x


def _projection_pieces(win_ref, zin_w, h_s, first_cols):
    n = win_ref.shape[1] // MXU_COLS
    early = list(range(first_cols[0] // MXU_COLS, first_cols[1] // MXU_COLS))
    for j in early + [j for j in range(n) if j not in early]:
        cs = slice(j * MXU_COLS, (j + 1) * MXU_COLS)
        zin_w[:, cs] = jnp.dot(h_s[...], win_ref[:, cs], preferred_element_type=jnp.float32)
        yield


def _mixer_stage2(zin_s, x_s, mod_ref, gpost_ref, cb_ref, lng_ref, lnb_ref, lb_ref, ng_ref, wout_ref, z_ref,
                  o_ref, ubuf_s, ush_s, cwb_s, st_s, mix_s, q_s, k_s, ex_s, hl_s, flag_s, zin_next, make_filler,
                  *, layer, depth, short_path):
    tile = x_s.shape[0]
    cw = CONV_HALO
    conv_width = cb_ref.shape[1]
    hgrn_width = lb_ref.shape[1]
    heads = hgrn_width // HEAD_DIM
    n_chunks = tile // CHUNK
    n_slots = n_chunks * heads
    n_conv = tile // CONV_ROWS
    o0 = 2 * conv_width
    f32, bf16 = jnp.float32, jnp.bfloat16

    early_cols = (0, o0 + 2 * hgrn_width)
    n_early = (early_cols[1] - early_cols[0]) // MXU_COLS
    n_late = zin_s.shape[1] // MXU_COLS - n_early
    assert n_early <= SUBLANES
    filler = make_filler(early_cols)
    u = zin_s[:, 0:conv_width] * _sigmoid(zin_s[:, conv_width:2 * conv_width])
    ubuf_s[cw:cw + tile, :] = u
    next(filler, None)
    for r in range(1, SUBLANES):
        ush_s[r - 1, SUBLANES:cw + tile, :] = ubuf_s[SUBLANES - r:cw + tile - r, :]
        if r < n_early:
            next(filler, None)

    def prepare_next_tile():
        rows = [lb_ref[i:i + 1, :] for i in range(depth)]
        m = functools.reduce(jnp.maximum, rows)
        e = [jnp.exp(r_ - m) for r_ in rows]
        tot = functools.reduce(jnp.add, e)
        lb = functools.reduce(jnp.add, e[1:layer + 1], jnp.zeros_like(tot)) / tot
        log_lb = jnp.log(lb)
        log_1m = jnp.log(1.0 - lb)
        span = jnp.zeros((1, hgrn_width), f32)
        for c in range(n_chunks):
            r0 = c * CHUNK
            q_s[r0:r0 + CHUNK, :] = _silu(zin_next[r0:r0 + CHUNK, o0:o0 + hgrn_width])
            z = zin_next[r0:r0 + CHUNK, o0 + hgrn_width:o0 + 2 * hgrn_width]
            log_sig = jnp.minimum(z, 0.0) - jnp.log(1.0 + jnp.exp(-jnp.abs(z)))
            t2 = log_1m + log_sig
            log_f = jnp.maximum(log_lb, t2) + jnp.log(1.0 + jnp.exp(-jnp.abs(log_lb - t2)))
            k_s[r0:r0 + CHUNK, :] = (1.0 - lb) * _sigmoid(-z)
            hi = log_f.astype(bf16)
            lo = (log_f - hi.astype(f32)).astype(bf16)
            hl = jnp.concatenate([hi, lo], axis=0)
            hl_s[c] = hl
            ex = jnp.dot(z_ref[0:_SHORT_ROWS, :], hl, preferred_element_type=f32)
            ex_s[c, 0:_SHORT_ROWS, :] = ex
            rel = jnp.abs(ex[_REL_BLOCK * CHUNK:(_REL_BLOCK + 1) * CHUNK, :])
            span = jnp.maximum(span, jnp.max(rel, axis=0, keepdims=True))
            for _ in range(n_late * (c + 1) // n_chunks - n_late * c // n_chunks):
                next(filler, None)
        for _ in filler:
            pass
        flag_s[0] = (jnp.max(span) <= MAX_SPLIT_EXPONENT).astype(jnp.int32)

    def conv_block(r):
        rb = CONV_ROWS
        acc = jnp.broadcast_to(cb_ref[...], (rb, conv_width)).reshape(rb // SUBLANES, SUBLANES, conv_width)
        for j in range(CONV_KERNEL):
            delay = CONV_KERNEL - 1 - j
            start = r * rb + cw - (delay // SUBLANES) * SUBLANES
            res = delay % SUBLANES
            src = ubuf_s[start:start + rb, :] if res == 0 else ush_s[res - 1, start:start + rb, :]
            acc = acc + cwb_s[j][None] * src.reshape(rb // SUBLANES, SUBLANES, conv_width)
        acc = acc.reshape(rb, conv_width)
        mu = jnp.mean(acc, axis=-1, keepdims=True)
        yield
        d = acc - mu
        var = jnp.mean(d * d, axis=-1, keepdims=True)
        yield
        y = d * lax.rsqrt(var + EPS) * lng_ref[...] + lnb_ref[...]
        mix_s[r * rb:(r + 1) * rb, 0:conv_width] = _silu(y).astype(bf16)

    def level_weight(c, n, ls):
        blk = _level_block(n)
        return jnp.exp(ex_s[c, blk * CHUNK:(blk + 1) * CHUNK, ls]).astype(bf16)

    def level_mask(n, tt, ss):
        return jnp.logical_and(((tt ^ ss) >> (n.bit_length() - 1)) == 1, tt > ss)

    def head(c, hd):
        r0 = c * CHUNK
        rs = slice(r0, r0 + CHUNK)
        ls = slice(hd * HEAD_DIM, (hd + 1) * HEAD_DIM)
        tt = lax.broadcasted_iota(jnp.int32, (CHUNK, CHUNK), 0)
        ss = lax.broadcasted_iota(jnp.int32, (CHUNK, CHUNK), 1)
        qh, kh = q_s[rs, ls], k_s[rs, ls]
        vcol = o0 + 2 * hgrn_width + hd * HEAD_DIM
        qb, kb, vb = qh.astype(bf16), kh.astype(bf16), zin_s[rs, vcol:vcol + HEAD_DIM].astype(bf16)
        b_cum = ex_s[c, 0:CHUNK, ls]
        levels = SPLIT_LEVELS if short_path else LEVELS
        if short_path:
            rel = ex_s[c, _REL_BLOCK * CHUNK:(_REL_BLOCK + 1) * CHUNK, ls]
            prods = [lax.dot_general((qh * jnp.exp(rel)).astype(bf16), (kh * jnp.exp(-rel)).astype(bf16), _NT,
                                     preferred_element_type=f32)]
        else:
            prods = [lax.dot_general(qb, kb, _NT, preferred_element_type=f32)]
        for n in levels:
            w = level_weight(c, n, ls)
            prods.append(lax.dot_general(qb * w, kb * w, _NT, preferred_element_type=f32))
        q_in = (qh * jnp.exp(b_cum)).astype(bf16)
        k_out = (kh * jnp.exp(ex_s[c, CHUNK:2 * CHUNK, ls])).astype(bf16)
        d_st = lax.dot_general(vb, k_out, _TN, preferred_element_type=f32)
        yield
        if short_path:
            inside = jnp.logical_and(tt // SPLIT_BLOCK == ss // SPLIT_BLOCK, tt >= ss)
        else:
            inside = tt == ss
        a = jnp.where(inside, prods[0], 0.0)
        for n, p in zip(levels, prods[1:]):
            a = jnp.where(level_mask(n, tt, ss), p, a)
        st = st_s[hd]
        o = (jnp.dot(a.astype(bf16), vb, preferred_element_type=f32)
             + lax.dot_general(q_in, st.astype(bf16), _NT, preferred_element_type=f32))
        st_s[hd] = st * jnp.exp(b_cum[CHUNK - 1:CHUNK, :]) + d_st
        yield
        ms = jnp.mean(o * o, axis=-1, keepdims=True)
        gcol = o0 + 3 * hgrn_width + hd * HEAD_DIM
        gate = _silu(zin_s[rs, gcol:gcol + HEAD_DIM])
        yield
        o = o * lax.rsqrt(ms + EPS) * ng_ref[...] * gate
        mix_s[rs, conv_width + hd * HEAD_DIM:conv_width + (hd + 1) * HEAD_DIM] = o.astype(bf16)

    def slots():
        if not short_path:
            for c in range(n_chunks):
                ex_s[c, _SHORT_ROWS:, :] = jnp.dot(z_ref[_SHORT_ROWS:, :], hl_s[c], preferred_element_type=f32)
        for s in range(n_slots):
            streams = [head(s // heads, s % heads)]
            streams += [conv_block(r) for r in range(s * n_conv // n_slots, (s + 1) * n_conv // n_slots)]
            _lockstep(streams)

    slots()
    ubuf_s[0:cw, :] = ubuf_s[tile:tile + cw, :]
    mixed = jnp.dot(mix_s[...], wout_ref[...], preferred_element_type=f32)
    gt1 = mod_ref[0][2:3]
    o_ref[0] = x_s[...] + _rms(mixed, gt1 * gpost_ref[...])
    prepare_next_tile()


def _mixer_kernel(xn_ref, modn_ref, modc_ref, gpre_ref, gpost_ref, win_ref, cw_ref, cb_ref, lng_ref, lnb_ref,
                  lb_ref, ng_ref, wout_ref, z_ref, o_ref, zin_a, zin_b, x_a, x_b, ubuf_s, ush_s, cwb_s, st_s,
                  mix_s, h_s, q_s, k_s, ex_s, hl_s, flag_s, *, layer, depth, tiles_per_row):
    g = pl.program_id(0)
    f32 = jnp.float32

    @pl.when(g == 0)
    def _():
        zin_b[...] = jnp.zeros(zin_b.shape, f32)
        x_b[...] = jnp.zeros(x_b.shape, f32)
        q_s[...] = jnp.zeros(q_s.shape, f32)
        k_s[...] = jnp.zeros(k_s.shape, f32)
        ex_s[...] = jnp.zeros(ex_s.shape, f32)
        hl_s[...] = jnp.zeros(hl_s.shape, hl_s.dtype)
        flag_s[0] = jnp.int32(1)
        for j in range(CONV_KERNEL):
            cwb_s[j] = jnp.broadcast_to(cw_ref[j:j + 1, :], (SUBLANES, cw_ref.shape[1]))

    @pl.when(jnp.maximum(g - 1, 0) % tiles_per_row == 0)
    def _():
        ubuf_s[0:CONV_HALO, :] = jnp.zeros((CONV_HALO, ubuf_s.shape[1]), f32)
        st_s[...] = jnp.zeros(st_s.shape, f32)

    def step(zin_w, x_w, zin_r, x_r, short_path):
        _projection_input(xn_ref, modn_ref, gpre_ref, x_w, h_s)
        _mixer_stage2(zin_r, x_r, modc_ref, gpost_ref, cb_ref, lng_ref, lnb_ref, lb_ref, ng_ref, wout_ref,
                      z_ref, o_ref, ubuf_s, ush_s, cwb_s, st_s, mix_s, q_s, k_s, ex_s, hl_s, flag_s, zin_w,
                      functools.partial(_projection_pieces, win_ref, zin_w, h_s),
                      layer=layer, depth=depth, short_path=short_path)

    short = flag_s[0] == 1
    even = g % 2 == 0
    for parity_even, bufs in ((True, (zin_a, x_a, zin_b, x_b)), (False, (zin_b, x_b, zin_a, x_a))):
        for short_path in (True, False):
            cond = jnp.logical_and(even == parity_even, short == short_path)
            pl.when(cond)(functools.partial(step, *bufs, short_path))


def _ffn_kernel(x_ref, mod_ref, gpre_ref, gpost_ref, w1_ref, w2_ref, o_ref):
    f32 = jnp.float32
    d_ff = w2_ref.shape[0]
    x = x_ref[0]
    mod = mod_ref[0]
    sh2, sc2, gt2 = mod[3:4], mod[4:5], mod[5:6]
    h = _rms(x, gpre_ref[...] * (1.0 + sc2)) + sh2
    gu = jnp.dot(h, w1_ref[...], preferred_element_type=f32)
    a = _silu(gu[:, 0:d_ff]) * gu[:, d_ff:2 * d_ff]
    y = jnp.dot(a, w2_ref[...], preferred_element_type=f32)
    o_ref[0] = x + _rms(y, gt2 * gpost_ref[...])


def _const_spec(shape):
    return pl.BlockSpec(shape, lambda *_: (0,) * len(shape))


def _modulation(c, w_ada, b_ada):
    depth, d, n = w_ada.shape
    b = c.shape[0]
    tn = 1024
    return pl.pallas_call(
        _mod_kernel,
        grid=(depth, n // tn),
        in_specs=[pl.BlockSpec((b, d), lambda l, j: (0, 0)),
                  pl.BlockSpec((1, d, tn), lambda l, j: (l, 0, j)),
                  pl.BlockSpec((1, 1, tn), lambda l, j: (l, 0, j))],
        out_specs=pl.BlockSpec((1, b, tn), lambda l, j: (l, 0, j)),
        out_shape=jax.ShapeDtypeStruct((depth, b, n), jnp.float32),
        compiler_params=pltpu.CompilerParams(dimension_semantics=("arbitrary", "arbitrary"),
                                             vmem_limit_bytes=VMEM_LIMIT_BYTES),
        name="adaln_mod",
    )(c, w_ada, b_ada.reshape(depth, 1, n))


def _mixer(x, mod, g_pre, g_post, w_in, conv_w, conv_b, ln_g, ln_b, lower_bounds, norm_g, w_out, zmat,
           *, layer, tile):
    b, s, d = x.shape
    in_width = w_in.shape[2]
    layer_slab = lambda w: pl.BlockSpec((None,) + w.shape[1:], lambda g: (layer, 0, 0),
                                        pipeline_mode=pl.Buffered(1))
    conv_width = conv_w.shape[1]
    depth, hgrn_width = lower_bounds.shape
    heads = hgrn_width // HEAD_DIM
    mix_width = w_out.shape[1]
    tiles_per_row = s // tile
    n_tiles = b * tiles_per_row
    nxt = lambda g: jnp.minimum(g, n_tiles - 1)
    cur = lambda g: jnp.maximum(g - 1, 0)
    kern = functools.partial(_mixer_kernel, layer=layer, depth=depth, tiles_per_row=tiles_per_row)
    out = pl.pallas_call(
        kern,
        grid=(n_tiles + 1,),
        in_specs=[pl.BlockSpec((1, tile, d), lambda g: (nxt(g), 0, 0)),
                  pl.BlockSpec((1, N_MOD, d), lambda g: (nxt(g) // tiles_per_row, 0, 0)),
                  pl.BlockSpec((1, N_MOD, d), lambda g: (cur(g) // tiles_per_row, 0, 0)),
                  _const_spec((1, d)), _const_spec((1, d)),
                  layer_slab(w_in),
                  _const_spec((CONV_KERNEL, conv_width)), _const_spec((1, conv_width)),
                  _const_spec((1, conv_width)), _const_spec((1, conv_width)),
                  _const_spec((depth, hgrn_width)), _const_spec((1, HEAD_DIM)),
                  layer_slab(w_out),
                  _const_spec(zmat.shape)],
        out_specs=pl.BlockSpec((1, tile, d), lambda g: (cur(g), 0, 0)),
        out_shape=jax.ShapeDtypeStruct((n_tiles, tile, d), jnp.float32),
        scratch_shapes=[pltpu.VMEM((tile, in_width), jnp.float32),
                        pltpu.VMEM((tile, in_width), jnp.float32),
                        pltpu.VMEM((tile, d), jnp.float32),
                        pltpu.VMEM((tile, d), jnp.float32),
                        pltpu.VMEM((tile + CONV_HALO, conv_width), jnp.float32),
                        pltpu.VMEM((SUBLANES - 1, tile + CONV_HALO, conv_width), jnp.float32),
                        pltpu.VMEM((CONV_KERNEL, SUBLANES, conv_width), jnp.float32),
                        pltpu.VMEM((heads, HEAD_DIM, HEAD_DIM), jnp.float32),
                        pltpu.VMEM((tile, mix_width), jnp.bfloat16),
                        pltpu.VMEM((tile, d), jnp.bfloat16),
                        pltpu.VMEM((tile, hgrn_width), jnp.float32),
                        pltpu.VMEM((tile, hgrn_width), jnp.float32),
                        pltpu.VMEM((tile // CHUNK, _ALL_ROWS, hgrn_width), jnp.float32),
                        pltpu.VMEM((tile // CHUNK, 2 * CHUNK, hgrn_width), jnp.bfloat16),
                        pltpu.SMEM((1,), jnp.int32)],
        compiler_params=pltpu.CompilerParams(dimension_semantics=("arbitrary",),
                                             vmem_limit_bytes=VMEM_LIMIT_BYTES),
        name=f"mixer_l{layer}",
    )(x.reshape(n_tiles, tile, d), mod, mod, g_pre, g_post, w_in, conv_w, conv_b, ln_g, ln_b, lower_bounds,
      norm_g, w_out, zmat)
    return out.reshape(b, s, d)


def _ffn(x, mod, g_pre, g_post, w1_all, w2_all, *, layer, tile):
    b, s, d = x.shape
    layer_slab = lambda w: pl.BlockSpec((None,) + w.shape[1:], lambda i, t: (layer, 0, 0),
                                        pipeline_mode=pl.Buffered(1))
    return pl.pallas_call(
        _ffn_kernel,
        grid=(b, s // tile),
        in_specs=[pl.BlockSpec((1, tile, d), lambda i, t: (i, t, 0)),
                  pl.BlockSpec((1, N_MOD, d), lambda i, t: (i, 0, 0)),
                  _const_spec((1, d)), _const_spec((1, d)),
                  layer_slab(w1_all), layer_slab(w2_all)],
        out_specs=pl.BlockSpec((1, tile, d), lambda i, t: (i, t, 0)),
        out_shape=jax.ShapeDtypeStruct((b, s, d), jnp.float32),
        compiler_params=pltpu.CompilerParams(dimension_semantics=("arbitrary", "arbitrary"),
                                             vmem_limit_bytes=VMEM_LIMIT_BYTES),
        name=f"ffn_l{layer}",
    )(x, mod, g_pre, g_post, w1_all, w2_all)


def kernel(x, c, w_ada, b_ada, g_pre_mix, g_post_mix, w_in, conv_w, conv_b, conv_ln_g, conv_ln_b,
           hgrn_lower_bounds, hgrn_norm_g, w_out, g_pre_ffn, g_post_ffn, w_ffn_in, w_ffn_out):
    depth = w_ada.shape[0]
    b, s, d = x.shape
    bf16 = jnp.bfloat16
    mod = _modulation(c, w_ada, b_ada).reshape(depth, b, N_MOD, d)
    zmat = jnp.asarray(_decay_matrix(), dtype=bf16)
    tile = min(256, s)
    w_in_bf, w_out_bf = w_in.astype(bf16), w_out.astype(bf16)
    for l in range(depth):
        row = lambda p: p[l][None, :]
        x = _mixer(x, mod[l], row(g_pre_mix), row(g_post_mix), w_in_bf, conv_w[l],
                   row(conv_b), row(conv_ln_g), row(conv_ln_b), hgrn_lower_bounds, row(hgrn_norm_g),
                   w_out_bf, zmat, layer=l, tile=tile)
        x = _ffn(x, mod[l], row(g_pre_ffn), row(g_post_ffn), w_ffn_in, w_ffn_out, layer=l, tile=tile)
    return x
```

```python
import functools

import numpy as np
import jax
import jax.numpy as jnp
from jax import lax
from jax.experimental import pallas as pl
from jax.experimental.pallas import tpu as pltpu

EPS = 1e-6
CONV_KERNEL = 31
CONV_HALO = 32
CONV_ROWS = 16
SUBLANES = 8
HEAD_DIM = 128
CHUNK = 64
MXU_COLS = 256
LEVELS = (32, 16, 8, 4, 2, 1)
SPLIT_BLOCK = 32
SPLIT_LEVELS = tuple(n for n in LEVELS if n >= SPLIT_BLOCK)
MAX_SPLIT_EXPONENT = 70.0
N_MOD = 6
VMEM_LIMIT_BYTES = 56 * 1024 * 1024

_NT = (((1,), (1,)), ((), ()))
_TN = (((0,), (0,)), ((), ()))


def _level_rows(n):
    t = np.arange(CHUNK)[:, None]
    j = np.arange(CHUNK)[None, :]
    p = (t // (2 * n)) * (2 * n) + n - 1
    upper = (t % (2 * n)) >= n
    return np.where(upper, (j > p) & (j <= t), (j > t) & (j <= p)).astype(np.float32)


def _decay_matrix():
    t = np.arange(CHUNK)[:, None]
    j = np.arange(CHUNK)[None, :]
    mid = (t // SPLIT_BLOCK) * SPLIT_BLOCK + SPLIT_BLOCK // 2 - 1
    rel = ((j > mid) & (j <= t)).astype(np.float32) - ((j > t) & (j <= mid)).astype(np.float32)
    blocks = [(j <= t).astype(np.float32), (j > t).astype(np.float32)]
    blocks += [_level_rows(n) for n in SPLIT_LEVELS] + [rel]
    blocks += [_level_rows(n) for n in LEVELS if n not in SPLIT_LEVELS]
    z = np.concatenate(blocks, axis=0)
    return np.concatenate([z, z], axis=1)


def _level_block(n):
    if n in SPLIT_LEVELS:
        return 2 + SPLIT_LEVELS.index(n)
    rest = [m for m in LEVELS if m not in SPLIT_LEVELS]
    return 3 + len(SPLIT_LEVELS) + rest.index(n)


_REL_BLOCK = 2 + len(SPLIT_LEVELS)
_SHORT_ROWS = (_REL_BLOCK + 1) * CHUNK
_ALL_ROWS = (3 + len(LEVELS)) * CHUNK


def _sigmoid(x):
    return 1.0 / (1.0 + jnp.exp(-x))


def _silu(x):
    return x * _sigmoid(x)


def _rms(x, g):
    return x * lax.rsqrt(jnp.mean(x * x, axis=-1, keepdims=True) + EPS) * g


def _mod_kernel(c_ref, w_ref, b_ref, o_ref):
    c = c_ref[...]
    o_ref[0] = jnp.dot(_silu(c), w_ref[0], preferred_element_type=jnp.float32) + b_ref[0]


def _lockstep(streams):
    active = list(streams)
    while active:
        for s in list(active):
            if next(s, active) is active:
                active.remove(s)


def _projection_input(x_ref, mod_ref, gpre_ref, x_w, h_s):
    x = x_ref[0]
    mod = mod_ref[0]
    sh1, sc1 = mod[0:1], mod[1:2]
    h_s[...] = (_rms(x, gpre_ref[...] * (1.0 + sc1)) + sh1).astype(jnp.bfloat16)
    x_w[...] = x


def _projection_pieces(win_ref, zin_w, h_s):
    for j in range(win_ref.shape[1] // MXU_COLS):
        cs = slice(j * MXU_COLS, (j + 1) * MXU_COLS)
        zin_w[:, cs] = jnp.dot(h_s[...], win_ref[:, cs], preferred_element_type=jnp.float32)
        yield


def _mixer_stage2(zin_s, x_s, mod_ref, gpost_ref, cb_ref, lng_ref, lnb_ref, lb_ref, ng_ref, wout_ref, z_ref,
                  o_ref, ubuf_s, ush_s, cwb_s, st_s, mix_s, q_s, k_s, ex_s, hl_s, flag_s, zin_next, make_filler,
                  *, layer, depth, short_path):
    tile = x_s.shape[0]
    cw = CONV_HALO
    conv_width = cb_ref.shape[1]
    hgrn_width = lb_ref.shape[1]
    heads = hgrn_width // HEAD_DIM
    n_chunks = tile // CHUNK
    n_slots = n_chunks * heads
    n_conv = tile // CONV_ROWS
    o0 = 2 * conv_width
    f32, bf16 = jnp.float32, jnp.bfloat16

    filler = make_filler()
    u = zin_s[:, 0:conv_width] * _sigmoid(zin_s[:, conv_width:2 * conv_width])
    ubuf_s[cw:cw + tile, :] = u
    next(filler, None)
    for r in range(1, SUBLANES):
        ush_s[r - 1, SUBLANES:cw + tile, :] = ubuf_s[SUBLANES - r:cw + tile - r, :]
        next(filler, None)

    for _ in filler:
        pass

    def prepare_next_tile():
        rows = [lb_ref[i:i + 1, :] for i in range(depth)]
        m = functools.reduce(jnp.maximum, rows)
        e = [jnp.exp(r_ - m) for r_ in rows]
        tot = functools.reduce(jnp.add, e)
        lb = functools.reduce(jnp.add, e[1:layer + 1], jnp.zeros_like(tot)) / tot
        log_lb = jnp.log(lb)
        log_1m = jnp.log(1.0 - lb)
        span = jnp.zeros((1, hgrn_width), f32)
        for c in range(n_chunks):
            r0 = c * CHUNK
            q_s[r0:r0 + CHUNK, :] = _silu(zin_next[r0:r0 + CHUNK, o0:o0 + hgrn_width])
            z = zin_next[r0:r0 + CHUNK, o0 + hgrn_width:o0 + 2 * hgrn_width]
            log_sig = jnp.minimum(z, 0.0) - jnp.log(1.0 + jnp.exp(-jnp.abs(z)))
            t2 = log_1m + log_sig
            log_f = jnp.maximum(log_lb, t2) + jnp.log(1.0 + jnp.exp(-jnp.abs(log_lb - t2)))
            k_s[r0:r0 + CHUNK, :] = (1.0 - lb) * _sigmoid(-z)
            hi = log_f.astype(bf16)
            lo = (log_f - hi.astype(f32)).astype(bf16)
            hl = jnp.concatenate([hi, lo], axis=0)
            hl_s[c] = hl
            ex = jnp.dot(z_ref[0:_SHORT_ROWS, :], hl, preferred_element_type=f32)
            ex_s[c, 0:_SHORT_ROWS, :] = ex
            rel = jnp.abs(ex[_REL_BLOCK * CHUNK:(_REL_BLOCK + 1) * CHUNK, :])
            span = jnp.maximum(span, jnp.max(rel, axis=0, keepdims=True))
        flag_s[0] = (jnp.max(span) <= MAX_SPLIT_EXPONENT).astype(jnp.int32)

    def conv_block(r):
        rb = CONV_ROWS
        acc = jnp.broadcast_to(cb_ref[...], (rb, conv_width)).reshape(rb // SUBLANES, SUBLANES, conv_width)
        for j in range(CONV_KERNEL):
            delay = CONV_KERNEL - 1 - j
            start = r * rb + cw - (delay // SUBLANES) * SUBLANES
            res = delay % SUBLANES
            src = ubuf_s[start:start + rb, :] if res == 0 else ush_s[res - 1, start:start + rb, :]
            acc = acc + cwb_s[j][None] * src.reshape(rb // SUBLANES, SUBLANES, conv_width)
        acc = acc.reshape(rb, conv_width)
        mu = jnp.mean(acc, axis=-1, keepdims=True)
        yield
        d = acc - mu
        var = jnp.mean(d * d, axis=-1, keepdims=True)
        yield
        y = d * lax.rsqrt(var + EPS) * lng_ref[...] + lnb_ref[...]
        mix_s[r * rb:(r + 1) * rb, 0:conv_width] = _silu(y).astype(bf16)

    def level_weight(c, n, ls):
        blk = _level_block(n)
        return jnp.exp(ex_s[c, blk * CHUNK:(blk + 1) * CHUNK, ls]).astype(bf16)

    def level_mask(n, tt, ss):
        return jnp.logical_and(((tt ^ ss) >> (n.bit_length() - 1)) == 1, tt > ss)

    def head(c, hd):
        r0 = c * CHUNK
        rs = slice(r0, r0 + CHUNK)
        ls = slice(hd * HEAD_DIM, (hd + 1) * HEAD_DIM)
        tt = lax.broadcasted_iota(jnp.int32, (CHUNK, CHUNK), 0)
        ss = lax.broadcasted_iota(jnp.int32, (CHUNK, CHUNK), 1)
        qh, kh = q_s[rs, ls], k_s[rs, ls]
        vcol = o0 + 2 * hgrn_width + hd * HEAD_DIM
        qb, kb, vb = qh.astype(bf16), kh.astype(bf16), zin_s[rs, vcol:vcol + HEAD_DIM].astype(bf16)
        b_cum = ex_s[c, 0:CHUNK, ls]
        levels = SPLIT_LEVELS if short_path else LEVELS
        if short_path:
            rel = ex_s[c, _REL_BLOCK * CHUNK:(_REL_BLOCK + 1) * CHUNK, ls]
            prods = [lax.dot_general((qh * jnp.exp(rel)).astype(bf16), (kh * jnp.exp(-rel)).astype(bf16), _NT,
                                     preferred_element_type=f32)]
        else:
            prods = [lax.dot_general(qb, kb, _NT, preferred_element_type=f32)]
        for n in levels:
            w = level_weight(c, n, ls)
            prods.append(lax.dot_general(qb * w, kb * w, _NT, preferred_element_type=f32))
        q_in = (qh * jnp.exp(b_cum)).astype(bf16)
        k_out = (kh * jnp.exp(ex_s[c, CHUNK:2 * CHUNK, ls])).astype(bf16)
        d_st = lax.dot_general(vb, k_out, _TN, preferred_element_type=f32)
        yield
        if short_path:
            inside = jnp.logical_and(tt // SPLIT_BLOCK == ss // SPLIT_BLOCK, tt >= ss)
        else:
            inside = tt == ss
        a = jnp.where(inside, prods[0], 0.0)
        for n, p in zip(levels, prods[1:]):
            a = jnp.where(level_mask(n, tt, ss), p, a)
        st = st_s[hd]
        o = (jnp.dot(a.astype(bf16), vb, preferred_element_type=f32)
             + lax.dot_general(q_in, st.astype(bf16), _NT, preferred_element_type=f32))
        st_s[hd] = st * jnp.exp(b_cum[CHUNK - 1:CHUNK, :]) + d_st
        yield
        ms = jnp.mean(o * o, axis=-1, keepdims=True)
        gcol = o0 + 3 * hgrn_width + hd * HEAD_DIM
        gate = _silu(zin_s[rs, gcol:gcol + HEAD_DIM])
        yield
        o = o * lax.rsqrt(ms + EPS) * ng_ref[...] * gate
        mix_s[rs, conv_width + hd * HEAD_DIM:conv_width + (hd + 1) * HEAD_DIM] = o.astype(bf16)

    def slots():
        if not short_path:
            for c in range(n_chunks):
                ex_s[c, _SHORT_ROWS:, :] = jnp.dot(z_ref[_SHORT_ROWS:, :], hl_s[c], preferred_element_type=f32)
        _lockstep([head(c, hd) for c in range(n_chunks) for hd in range(heads)])
        for r in range(n_conv):
            _lockstep([conv_block(r)])

    slots()
    ubuf_s[0:cw, :] = ubuf_s[tile:tile + cw, :]
    mixed = jnp.dot(mix_s[...], wout_ref[...], preferred_element_type=f32)
    gt1 = mod_ref[0][2:3]
    o_ref[0] = x_s[...] + _rms(mixed, gt1 * gpost_ref[...])
    prepare_next_tile()


def _mixer_kernel(xn_ref, modn_ref, modc_ref, gpre_ref, gpost_ref, win_ref, cw_ref, cb_ref, lng_ref, lnb_ref,
                  lb_ref, ng_ref, wout_ref, z_ref, o_ref, zin_a, zin_b, x_a, x_b, ubuf_s, ush_s, cwb_s, st_s,
                  mix_s, h_s, q_s, k_s, ex_s, hl_s, flag_s, *, layer, depth, tiles_per_row):
    g = pl.program_id(0)
    f32 = jnp.float32

    @pl.when(g == 0)
    def _():
        zin_b[...] = jnp.zeros(zin_b.shape, f32)
        x_b[...] = jnp.zeros(x_b.shape, f32)
        q_s[...] = jnp.zeros(q_s.shape, f32)
        k_s[...] = jnp.zeros(k_s.shape, f32)
        ex_s[...] = jnp.zeros(ex_s.shape, f32)
        hl_s[...] = jnp.zeros(hl_s.shape, hl_s.dtype)
        flag_s[0] = jnp.int32(1)
        for j in range(CONV_KERNEL):
            cwb_s[j] = jnp.broadcast_to(cw_ref[j:j + 1, :], (SUBLANES, cw_ref.shape[1]))

    @pl.when(jnp.maximum(g - 1, 0) % tiles_per_row == 0)
    def _():
        ubuf_s[0:CONV_HALO, :] = jnp.zeros((CONV_HALO, ubuf_s.shape[1]), f32)
        st_s[...] = jnp.zeros(st_s.shape, f32)

    def step(zin_w, x_w, zin_r, x_r, short_path):
        _projection_input(xn_ref, modn_ref, gpre_ref, x_w, h_s)
        _mixer_stage2(zin_r, x_r, modc_ref, gpost_ref, cb_ref, lng_ref, lnb_ref, lb_ref, ng_ref, wout_ref,
                      z_ref, o_ref, ubuf_s, ush_s, cwb_s, st_s, mix_s, q_s, k_s, ex_s, hl_s, flag_s, zin_w,
                      functools.partial(_projection_pieces, win_ref, zin_w, h_s),
                      layer=layer, depth=depth, short_path=short_path)

    short = flag_s[0] == 1
    even = g % 2 == 0
    for parity_even, bufs in ((True, (zin_a, x_a, zin_b, x_b)), (False, (zin_b, x_b, zin_a, x_a))):
        for short_path in (True, False):
            cond = jnp.logical_and(even == parity_even, short == short_path)
            pl.when(cond)(functools.partial(step, *bufs, short_path))


def _ffn_kernel(x_ref, mod_ref, gpre_ref, gpost_ref, w1_ref, w2_ref, o_ref):
    f32 = jnp.float32
    d_ff = w2_ref.shape[0]
    x = x_ref[0]
    mod = mod_ref[0]
    sh2, sc2, gt2 = mod[3:4], mod[4:5], mod[5:6]
    h = _rms(x, gpre_ref[...] * (1.0 + sc2)) + sh2
    gu = jnp.dot(h, w1_ref[...], preferred_element_type=f32)
    a = _silu(gu[:, 0:d_ff]) * gu[:, d_ff:2 * d_ff]
    y = jnp.dot(a, w2_ref[...], preferred_element_type=f32)
    o_ref[0] = x + _rms(y, gt2 * gpost_ref[...])


def _const_spec(shape):
    return pl.BlockSpec(shape, lambda *_: (0,) * len(shape))


def _modulation(c, w_ada, b_ada):
    depth, d, n = w_ada.shape
    b = c.shape[0]
    tn = 1024
    return pl.pallas_call(
        _mod_kernel,
        grid=(depth, n // tn),
        in_specs=[pl.BlockSpec((b, d), lambda l, j: (0, 0)),
                  pl.BlockSpec((1, d, tn), lambda l, j: (l, 0, j)),
                  pl.BlockSpec((1, 1, tn), lambda l, j: (l, 0, j))],
        out_specs=pl.BlockSpec((1, b, tn), lambda l, j: (l, 0, j)),
        out_shape=jax.ShapeDtypeStruct((depth, b, n), jnp.float32),
        compiler_params=pltpu.CompilerParams(dimension_semantics=("arbitrary", "arbitrary"),
                                             vmem_limit_bytes=VMEM_LIMIT_BYTES),
        name="adaln_mod",
    )(c, w_ada, b_ada.reshape(depth, 1, n))


def _mixer(x, mod, g_pre, g_post, w_in, conv_w, conv_b, ln_g, ln_b, lower_bounds, norm_g, w_out, zmat,
           *, layer, tile):
    b, s, d = x.shape
    in_width = w_in.shape[2]
    layer_slab = lambda w: pl.BlockSpec((None,) + w.shape[1:], lambda g: (layer, 0, 0),
                                        pipeline_mode=pl.Buffered(1))
    conv_width = conv_w.shape[1]
    depth, hgrn_width = lower_bounds.shape
    heads = hgrn_width // HEAD_DIM
    mix_width = w_out.shape[1]
    tiles_per_row = s // tile
    n_tiles = b * tiles_per_row
    nxt = lambda g: jnp.minimum(g, n_tiles - 1)
    cur = lambda g: jnp.maximum(g - 1, 0)
    kern = functools.partial(_mixer_kernel, layer=layer, depth=depth, tiles_per_row=tiles_per_row)
    out = pl.pallas_call(
        kern,
        grid=(n_tiles + 1,),
        in_specs=[pl.BlockSpec((1, tile, d), lambda g: (nxt(g), 0, 0)),
                  pl.BlockSpec((1, N_MOD, d), lambda g: (nxt(g) // tiles_per_row, 0, 0)),
                  pl.BlockSpec((1, N_MOD, d), lambda g: (cur(g) // tiles_per_row, 0, 0)),
                  _const_spec((1, d)), _const_spec((1, d)),
                  layer_slab(w_in),
                  _const_spec((CONV_KERNEL, conv_width)), _const_spec((1, conv_width)),
                  _const_spec((1, conv_width)), _const_spec((1, conv_width)),
                  _const_spec((depth, hgrn_width)), _const_spec((1, HEAD_DIM)),
                  layer_slab(w_out),
                  _const_spec(zmat.shape)],
        out_specs=pl.BlockSpec((1, tile, d), lambda g: (cur(g), 0, 0)),
        out_shape=jax.ShapeDtypeStruct((n_tiles, tile, d), jnp.float32),
        scratch_shapes=[pltpu.VMEM((tile, in_width), jnp.float32),
                        pltpu.VMEM((tile, in_width), jnp.float32),
                        pltpu.VMEM((tile, d), jnp.float32),
                        pltpu.VMEM((tile, d), jnp.float32),
                        pltpu.VMEM((tile + CONV_HALO, conv_width), jnp.float32),
                        pltpu.VMEM((SUBLANES - 1, tile + CONV_HALO, conv_width), jnp.float32),
                        pltpu.VMEM((CONV_KERNEL, SUBLANES, conv_width), jnp.float32),
                        pltpu.VMEM((heads, HEAD_DIM, HEAD_DIM), jnp.float32),
                        pltpu.VMEM((tile, mix_width), jnp.bfloat16),
                        pltpu.VMEM((tile, d), jnp.bfloat16),
                        pltpu.VMEM((tile, hgrn_width), jnp.float32),
                        pltpu.VMEM((tile, hgrn_width), jnp.float32),
                        pltpu.VMEM((tile // CHUNK, _ALL_ROWS, hgrn_width), jnp.float32),
                        pltpu.VMEM((tile // CHUNK, 2 * CHUNK, hgrn_width), jnp.bfloat16),
                        pltpu.SMEM((1,), jnp.int32)],
        compiler_params=pltpu.CompilerParams(dimension_semantics=("arbitrary",),
                                             vmem_limit_bytes=VMEM_LIMIT_BYTES),
        name=f"mixer_l{layer}",
    )(x.reshape(n_tiles, tile, d), mod, mod, g_pre, g_post, w_in, conv_w, conv_b, ln_g, ln_b, lower_bounds,
      norm_g, w_out, zmat)
    return out.reshape(b, s, d)


def _ffn(x, mod, g_pre, g_post, w1_all, w2_all, *, layer, tile):
    b, s, d = x.shape
    layer_slab = lambda w: pl.BlockSpec((None,) + w.shape[1:], lambda i, t: (layer, 0, 0),
                                        pipeline_mode=pl.Buffered(1))
    return pl.pallas_call(
        _ffn_kernel,
        grid=(b, s // tile),
        in_specs=[pl.BlockSpec((1, tile, d), lambda i, t: (i, t, 0)),
                  pl.BlockSpec((1, N_MOD, d), lambda i, t: (i, 0, 0)),
                  _const_spec((1, d)), _const_spec((1, d)),
                  layer_slab(w1_all), layer_slab(w2_all)],
        out_specs=pl.BlockSpec((1, tile, d), lambda i, t: (i, t, 0)),
        out_shape=jax.ShapeDtypeStruct((b, s, d), jnp.float32),
        compiler_params=pltpu.CompilerParams(dimension_semantics=("arbitrary", "arbitrary"),
                                             vmem_limit_bytes=VMEM_LIMIT_BYTES),
        name=f"ffn_l{layer}",
    )(x, mod, g_pre, g_post, w1_all, w2_all)


def kernel(x, c, w_ada, b_ada, g_pre_mix, g_post_mix, w_in, conv_w, conv_b, conv_ln_g, conv_ln_b,
           hgrn_lower_bounds, hgrn_norm_g, w_out, g_pre_ffn, g_post_ffn, w_ffn_in, w_ffn_out):
    depth = w_ada.shape[0]
    b, s, d = x.shape
    bf16 = jnp.bfloat16
    mod = _modulation(c, w_ada, b_ada).reshape(depth, b, N_MOD, d)
    zmat = jnp.asarray(_decay_matrix(), dtype=bf16)
    tile = min(256, s)
    w_in_bf, w_out_bf = w_in.astype(bf16), w_out.astype(bf16)
    for l in range(depth):
        row = lambda p: p[l][None, :]
        x = _mixer(x, mod[l], row(g_pre_mix), row(g_post_mix), w_in_bf, conv_w[l],
                   row(conv_b), row(conv_ln_g), row(conv_ln_b), hgrn_lower_bounds, row(hgrn_norm_g),
                   w_out_bf, zmat, layer=l, tile=tile)
        x = _ffn(x, mod[l], row(g_pre_ffn), row(g_post_ffn), w_ffn_in, w_ffn_out, layer=l, tile=tile)
    return x
```
